```python
import jax
import jax.numpy as jnp
from jax import lax
import numpy as np

D_MODEL = 1024
BATCH = 2
SEQ = 8192
DEPTH = 4

BRANCH_WIDTH = D_MODEL // 2
N_BRANCHES = 3
GM_CHUNK = 128
GM_GROUPS = 8
GM_GROUP_DIM = BRANCH_WIDTH // GM_GROUPS
DN_HEADS = 4
DN_HEAD_DIM = BRANCH_WIDTH // DN_HEADS
DN_CHUNK = 64
CONV_WIDTH = 4
SB_HEADS = 8
SB_HEAD_DIM = BRANCH_WIDTH // SB_HEADS
SB_BLOCK = 128
D_FF = 4 * D_MODEL
NORM_EPS = 1e-6
PROJ_SIZES = (BRANCH_WIDTH, BRANCH_WIDTH, BRANCH_WIDTH, BRANCH_WIDTH, BRANCH_WIDTH, BRANCH_WIDTH, DN_HEADS, DN_HEADS, BRANCH_WIDTH, BRANCH_WIDTH, BRANCH_WIDTH, N_BRANCHES * D_MODEL)
P_IN = 9 * BRANCH_WIDTH + 2 * DN_HEADS + N_BRANCHES * D_MODEL

kernel_name = "hybrid_gated_gmlp_deltanet_stickbreaking"


def rms_norm(x, g):
    xf = x.astype(jnp.float32)
    y = xf * lax.rsqrt(jnp.mean(xf * xf, axis=-1, keepdims=True) + NORM_EPS)
    return (y * g.astype(jnp.float32)).astype(x.dtype)


def layer_norm(x, g, b):
    xf = x.astype(jnp.float32)
    xc = xf - jnp.mean(xf, axis=-1, keepdims=True)
    y = xc * lax.rsqrt(jnp.mean(xc * xc, axis=-1, keepdims=True) + NORM_EPS)
    return (y * g.astype(jnp.float32) + b.astype(jnp.float32)).astype(x.dtype)


def l2_normalize(x):
    xf = x.astype(jnp.float32)
    return xf * lax.rsqrt(jnp.sum(xf * xf, axis=-1, keepdims=True) + NORM_EPS)


def split_columns(p):
    idx = []
    off = 0
    for s in PROJ_SIZES[:-1]:
        off += s
        idx.append(off)
    return jnp.split(p, idx, axis=-1)


def chunked_spatial_gating(u, v, w_s, b_s):
    bsz, s, _ = u.shape
    n = s // GM_CHUNK
    vh = v.reshape(bsz, n, GM_CHUNK, GM_GROUPS, GM_GROUP_DIM)
    causal = jnp.tril(jnp.ones((GM_CHUNK, GM_CHUNK), dtype=bool))
    w = jnp.where(causal[None], w_s, 0).astype(v.dtype)
    mixed = jnp.einsum('gts,bnsgc->bntgc', w, vh) + b_s.T.astype(v.dtype)[:, :, None]
    return u * mixed.reshape(bsz, s, BRANCH_WIDTH)


def causal_depthwise_conv(x, w):
    c = x.shape[-1]
    return lax.conv_general_dilated(
        x, w.astype(x.dtype)[:, None, :], window_strides=(1,),
        padding=[(CONV_WIDTH - 1, 0)], dimension_numbers=('NWC', 'WIO', 'NWC'),
        feature_group_count=c)


def gated_delta_rule(q, k, v, beta, g):
    f32 = jnp.float32
    bsz, s, h, dk = q.shape
    dv = v.shape[-1]
    c = DN_CHUNK
    n = s // c
    q = q.astype(f32) * (dk ** -0.5)

    def to_chunks(t):
        return t.astype(f32).reshape(bsz, n, c, h, -1).transpose(0, 3, 1, 2, 4)

    qc, kc, vc = to_chunks(q), to_chunks(k), to_chunks(v)
    bc = beta.astype(f32).reshape(bsz, n, c, h).transpose(0, 3, 1, 2)
    gcum = lax.cumsum(g.astype(f32).reshape(bsz, n, c, h).transpose(0, 3, 1, 2), axis=3)
    tri = jnp.tril(jnp.ones((c, c), dtype=bool))
    strict = jnp.tril(jnp.ones((c, c), dtype=bool), -1)
    decay = jnp.exp(jnp.where(tri, gcum[..., :, None] - gcum[..., None, :], -jnp.inf))
    kk = jnp.einsum('bhnid,bhnjd->bhnij', kc, kc)
    a_strict = jnp.where(strict, bc[..., :, None] * kk * decay, 0.0)
    lhs = a_strict + jnp.eye(c, dtype=f32)
    rhs = jnp.concatenate([vc * bc[..., None], kc * (bc * jnp.exp(gcum))[..., None]], axis=-1)
    sol = lax.linalg.triangular_solve(lhs, rhs, left_side=True, lower=True, unit_diagonal=True)
    u_val, w_kd = sol[..., :dv], sol[..., dv:]
    qk = jnp.where(tri, jnp.einsum('bhnid,bhnjd->bhnij', qc, kc) * decay, 0.0)
    q_dec = qc * jnp.exp(gcum)[..., None]
    g_last = gcum[..., -1]
    k_tail = kc * jnp.exp(g_last[..., None] - gcum)[..., None]

    def step(state, inp):
        u_i, w_i, qd_i, qk_i, kt_i, gl_i = inp
        v_new = u_i - jnp.einsum('bhcd,bhde->bhce', w_i, state)
        o = jnp.einsum('bhcd,bhde->bhce', qd_i, state) + jnp.einsum('bhcj,bhje->bhce', qk_i, v_new)
        state = state * jnp.exp(gl_i)[..., None, None] + jnp.einsum('bhcd,bhce->bhde', kt_i, v_new)
        return state, o

    xs = tuple(jnp.moveaxis(t, 2, 0) for t in (u_val, w_kd, q_dec, qk, k_tail, g_last))
    s0 = jnp.zeros((bsz, h, dk, dv), f32)
    _, o = lax.scan(step, s0, xs)
    return o.transpose(1, 0, 3, 2, 4).reshape(bsz, s, h, dv)


def stick_breaking_attention(q, k, v):
    f32 = jnp.float32
    bsz, s, h, dh = q.shape
    nb = s // SB_BLOCK
    qb = q.astype(f32).reshape(bsz, nb, SB_BLOCK, h, dh).transpose(1, 0, 2, 3, 4)
    kf = k.astype(f32)
    vf = v.astype(f32)
    kpos = jnp.arange(s)
    scale = dh ** -0.5

    def block(args):
        qi, i = args
        z = jnp.einsum('bthd,bshd->bhts', qi, kf) * scale
        tpos = i * SB_BLOCK + jnp.arange(SB_BLOCK)
        causal = kpos[None, :] < tpos[:, None]
        log_keep = jnp.where(causal, jax.nn.log_sigmoid(-z), 0.0)
        later = lax.cumsum(log_keep, axis=3, reverse=True) - log_keep
        w = jnp.where(causal, jnp.exp(jax.nn.log_sigmoid(z) + later), 0.0)
        return jnp.einsum('bhts,bshd->bthd', w, vf)

    o = lax.map(block, (qb, jnp.arange(nb)))
    return o.transpose(1, 0, 2, 3, 4).reshape(bsz, s, h * dh).astype(q.dtype)


def setup_inputs(seed: int = 0) -> dict:
    key = jax.random.key(seed)
    ks = jax.random.split(key, 16)
    f32 = jnp.float32
    nrm = lambda k, shp: jax.random.normal(k, shp, f32)
    x = nrm(ks[0], (BATCH, SEQ, D_MODEL))
    norm_g = 1.0 + 0.05 * nrm(ks[1], (DEPTH, 4, D_MODEL))
    w_in = nrm(ks[2], (DEPTH, D_MODEL, P_IN)) * D_MODEL ** -0.5
    b_in = 0.01 * nrm(ks[3], (DEPTH, P_IN))
    sgu_ln_g = 1.0 + 0.05 * nrm(ks[4], (DEPTH, BRANCH_WIDTH))
    sgu_ln_b = 0.01 * nrm(ks[5], (DEPTH, BRANCH_WIDTH))
    w_spatial = nrm(ks[6], (DEPTH, GM_GROUPS, GM_CHUNK, GM_CHUNK)) * GM_CHUNK ** -0.5
    b_spatial = 1.0 + 0.05 * nrm(ks[7], (DEPTH, GM_GROUPS, GM_CHUNK))
    conv_w = nrm(ks[8], (DEPTH, CONV_WIDTH, 3 * BRANCH_WIDTH)) * CONV_WIDTH ** -0.5
    a_log = jnp.log(jax.random.uniform(ks[9], (DEPTH, DN_HEADS), f32, 1.0, 16.0))
    dt = jnp.exp(jax.random.uniform(ks[10], (DEPTH, DN_HEADS), f32, float(np.log(1e-3)), float(np.log(1e-1))))
    dt_bias = dt + jnp.log(-jnp.expm1(-dt))
    dn_norm_g = 1.0 + 0.05 * nrm(ks[11], (DEPTH, DN_HEAD_DIM))
    w_branch = nrm(ks[12], (DEPTH, N_BRANCHES, BRANCH_WIDTH, D_MODEL)) * BRANCH_WIDTH ** -0.5
    w_out = nrm(ks[13], (DEPTH, D_MODEL, D_MODEL)) * D_MODEL ** -0.5
    w_ff1 = nrm(ks[14], (DEPTH, D_MODEL, D_FF)) * D_MODEL ** -0.5
    w_ff2 = nrm(ks[15], (DEPTH, D_FF, D_MODEL)) * D_FF ** -0.5
    return {"x": x, "norm_g": norm_g, "w_in": w_in, "b_in": b_in,
            "sgu_ln_g": sgu_ln_g, "sgu_ln_b": sgu_ln_b, "w_spatial": w_spatial, "b_spatial": b_spatial,
            "conv_w": conv_w, "a_log": a_log, "dt_bias": dt_bias, "dn_norm_g": dn_norm_g,
            "w_branch": w_branch, "w_out": w_out, "w_ff1": w_ff1, "w_ff2": w_ff2}


def reference(x, norm_g, w_in, b_in, sgu_ln_g, sgu_ln_b, w_spatial, b_spatial, conv_w, a_log, dt_bias, dn_norm_g, w_branch, w_out, w_ff1, w_ff2):
    bsz, s, _ = x.shape
    for l in range(DEPTH):
        h = rms_norm(x, norm_g[l, 0])
        p = h @ w_in[l] + b_in[l]
        (a_u, a_v, b_q, b_k, b_v, b_z, b_beta, b_dec, c_q, c_k, c_v, gates) = split_columns(p)

        a_u = jax.nn.gelu(a_u, approximate=False)
        a_v = layer_norm(jax.nn.gelu(a_v, approximate=False), sgu_ln_g[l], sgu_ln_b[l])
        y_a = chunked_spatial_gating(a_u, a_v, w_spatial[l], b_spatial[l])

        qkv = jax.nn.silu(causal_depthwise_conv(jnp.concatenate([b_q, b_k, b_v], axis=-1), conv_w[l]))
        dq, dk_, dv_ = jnp.split(qkv, 3, axis=-1)
        dq = l2_normalize(dq.reshape(bsz, s, DN_HEADS, DN_HEAD_DIM))
        dk_ = l2_normalize(dk_.reshape(bsz, s, DN_HEADS, DN_HEAD_DIM))
        dv_ = dv_.reshape(bsz, s, DN_HEADS, DN_HEAD_DIM)
        beta = jax.nn.sigmoid(b_beta.astype(jnp.float32))
        g = -jnp.exp(a_log[l].astype(jnp.float32)) * jax.nn.softplus(b_dec.astype(jnp.float32) + dt_bias[l].astype(jnp.float32))
        o = gated_delta_rule(dq, dk_, dv_, beta, g).astype(x.dtype)
        o = rms_norm(o, dn_norm_g[l]) * jax.nn.silu(b_z.reshape(bsz, s, DN_HEADS, DN_HEAD_DIM))
        y_b = o.reshape(bsz, s, BRANCH_WIDTH)

        y_c = stick_breaking_attention(c_q.reshape(bsz, s, SB_HEADS, SB_HEAD_DIM),
                                       c_k.reshape(bsz, s, SB_HEADS, SB_HEAD_DIM),
                                       c_v.reshape(bsz, s, SB_HEADS, SB_HEAD_DIM))

        branches = jnp.stack([y_a, y_b, y_c], axis=0)
        proj = jnp.einsum('nbsw,nwd->bsnd', branches, w_branch[l])
        gate = jax.nn.sigmoid(gates.reshape(bsz, s, N_BRANCHES, D_MODEL))
        mixed = jnp.sum(gate * proj, axis=2) @ w_out[l]
        x = x + rms_norm(mixed, norm_g[l, 1])

        h = rms_norm(x, norm_g[l, 2])
        f = jnp.square(jax.nn.relu(h @ w_ff1[l])) @ w_ff2[l]
        x = x + rms_norm(f, norm_g[l, 3])
    return x
```

```python
import functools

import jax
import jax.numpy as jnp
from jax import lax
from jax.experimental import pallas as pl
from jax.experimental.pallas import tpu as pltpu

F32 = jnp.float32
BF16 = jnp.bfloat16
NORM_EPS = 1e-6

LANES = 128
MIB = 1024 * 1024

BRANCH_W = 512
GM_CHUNK = 128
GM_GROUPS = 8
DN_HEADS = 4
DN_DIM = 128
DN_CHUNK = 64
CONV_W = 4
SB_HEADS = 8
SB_DIM = 64
N_BRANCH = 3

COL_AU, COL_AV, COL_BQ, COL_BK, COL_BV, COL_BZ, COL_CQ, COL_CV, COL_GATE = (
    0, 512, 1024, 1536, 2048, 2560, 3072, 3584, 4096)


def _cparams(sem, vmem_mib):
    return pltpu.CompilerParams(dimension_semantics=sem, vmem_limit_bytes=vmem_mib * MIB)


def _const_spec(shape):
    nd = len(shape)
    return pl.BlockSpec(shape, lambda *_: (0,) * nd, pipeline_mode=pl.Buffered(1))


def _rms(x, g):
    return x * lax.rsqrt(jnp.mean(x * x, axis=-1, keepdims=True) + NORM_EPS) * g


def _sigmoid(x):
    return 1.0 / (1.0 + jnp.exp(-x))


def _softplus(x):
    return jnp.maximum(x, 0.0) + jnp.log(1.0 + jnp.exp(-jnp.abs(x)))


def _gelu(x):
    return 0.5 * x * (1.0 + lax.erf(x * (2.0 ** -0.5)))


def _dot(a, b):
    return jnp.dot(a, b, preferred_element_type=F32)


def _dot_nt(a, b):
    return lax.dot_general(a, b, (((1,), (1,)), ((), ())), preferred_element_type=F32)


def _dot_tn(a, b):
    return lax.dot_general(a, b, (((0,), (0,)), ((), ())), preferred_element_type=F32)


IN_TM = 512
IN_CW = 1024


def _in_proj_kernel(x_ref, g_ref, w_ref, b_ref, wbd_ref, bbd_ref, wkt_ref, bk_ref,
                    p_ref, bd_ref, kt_ref):
    hb = _rms(x_ref[0], g_ref[...]).astype(BF16)
    p_cols = w_ref.shape[1]
    for c in range(0, p_cols, IN_CW):
        acc = _dot(hb, w_ref[:, c:c + IN_CW]) + b_ref[:, c:c + IN_CW]
        p_ref[0, :, c:c + IN_CW] = acc.astype(BF16)
    bd_ref[0] = _dot(hb, wbd_ref[...]) + bbd_ref[...]
    kt = _dot_nt(wkt_ref[...], hb) + bk_ref[...]
    for hp in range(SB_HEADS // 2):
        for jb in range(IN_TM // LANES):
            kt_ref[0, hp, jb] = kt[hp * LANES:(hp + 1) * LANES,
                                   jb * LANES:(jb + 1) * LANES].astype(BF16)


def _in_proj(x, g, w_main, b_main, w_bd, b_bd, w_kt, b_k):
    bsz, s, d = x.shape
    pm = w_main.shape[1]
    n_hp = SB_HEADS // 2
    grid = (bsz, s // IN_TM)
    return pl.pallas_call(
        _in_proj_kernel,
        grid=grid,
        in_specs=[
            pl.BlockSpec((1, IN_TM, d), lambda b, i: (b, i, 0)),
            _const_spec((1, d)),
            _const_spec((d, pm)),
            _const_spec((1, pm)),
            _const_spec((d, LANES)),
            _const_spec((1, LANES)),
            _const_spec((BRANCH_W, d)),
            _const_spec((BRANCH_W, 1)),
        ],
        out_specs=[
            pl.BlockSpec((1, IN_TM, pm), lambda b, i: (b, i, 0)),
            pl.BlockSpec((1, IN_TM, LANES), lambda b, i: (b, i, 0)),
            pl.BlockSpec((1, n_hp, IN_TM // LANES, LANES, LANES), lambda b, i: (b, 0, i, 0, 0)),
        ],
        out_shape=[
            jax.ShapeDtypeStruct((bsz, s, pm), BF16),
            jax.ShapeDtypeStruct((bsz, s, LANES), F32),
            jax.ShapeDtypeStruct((bsz, n_hp, s // LANES, LANES, LANES), BF16),
        ],
        compiler_params=_cparams(("parallel", "parallel"), 56),
        name="in_proj",
    )(x, g, w_main, b_main, w_bd, b_bd, w_kt, b_k)


SGU_T = 256


def _sgu_kernel(u_ref, v_ref, ws_ref, bs_ref, lng_ref, lnb_ref, o_ref):
    u = _gelu(u_ref[0].astype(F32))
    v = _gelu(v_ref[0].astype(F32))
    vc = v - jnp.mean(v, axis=-1, keepdims=True)
    v = vc * lax.rsqrt(jnp.mean(vc * vc, axis=-1, keepdims=True) + NORM_EPS)
    v = (v * lng_ref[...] + lnb_ref[...]).astype(BF16)
    row = lax.broadcasted_iota(jnp.int32, (GM_CHUNK, GM_CHUNK), 0)
    col = lax.broadcasted_iota(jnp.int32, (GM_CHUNK, GM_CHUNK), 1)
    causal = col <= row
    first_half = col < (LANES // 2)
    ws = [jnp.where(causal, ws_ref[g], 0.0).astype(BF16) for g in range(GM_GROUPS)]
    for c in range(SGU_T // GM_CHUNK):
        r0 = c * GM_CHUNK
        for p in range(GM_GROUPS // 2):
            v2 = v[r0:r0 + GM_CHUNK, p * LANES:(p + 1) * LANES]
            mixed = jnp.where(first_half, _dot(ws[2 * p], v2), _dot(ws[2 * p + 1], v2))
            mixed = mixed + bs_ref[:, p * LANES:(p + 1) * LANES]
            y = u[r0:r0 + GM_CHUNK, p * LANES:(p + 1) * LANES] * mixed
            o_ref[0, r0:r0 + GM_CHUNK, p * LANES:(p + 1) * LANES] = y.astype(BF16)


def _sgu(p_main, ws, bs_full, lng, lnb):
    bsz, s, _ = p_main.shape
    w = BRANCH_W
    return pl.pallas_call(
        _sgu_kernel,
        grid=(bsz, s // SGU_T),
        in_specs=[
            pl.BlockSpec((1, SGU_T, w), lambda b, i: (b, i, COL_AU // w)),
            pl.BlockSpec((1, SGU_T, w), lambda b, i: (b, i, COL_AV // w)),
            _const_spec((GM_GROUPS, GM_CHUNK, GM_CHUNK)),
            _const_spec((GM_CHUNK, w)),
            _const_spec((1, w)),
            _const_spec((1, w)),
        ],
        out_specs=pl.BlockSpec((1, SGU_T, w), lambda b, i: (b, i, 0)),
        out_shape=jax.ShapeDtypeStruct((bsz, s, w), BF16),
        compiler_params=_cparams(("parallel", "parallel"), 32),
        name="sgu",
    )(p_main, p_main, ws, bs_full, lng, lnb)


DN_T = 128
HALO = 8


def _dn_kernel(q_ref, k_ref, v_ref, z_ref, qp_ref, kp_ref, vp_ref, bd_ref,
               cw_ref, alog_ref, dtb_ref, ng_ref, o_ref, state_ref):
    sb = pl.program_id(1)

    @pl.when(sb == 0)
    def _():
        state_ref[...] = jnp.zeros_like(state_ref)

    not_first = (sb > 0).astype(F32)

    def conv_silu(x_ref, xp_ref, part):
        x = x_ref[0].astype(F32)
        xp = xp_ref[0].astype(F32) * not_first
        xc = jnp.concatenate([xp, x], axis=0)
        y = None
        for i in range(CONV_W):
            off = HALO - (CONV_W - 1) + i
            term = cw_ref[i:i + 1, part * BRANCH_W:(part + 1) * BRANCH_W] * xc[off:off + DN_T]
            y = term if y is None else y + term
        return y * _sigmoid(y)

    q = conv_silu(q_ref, qp_ref, 0)
    k = conv_silu(k_ref, kp_ref, 1)
    v = conv_silu(v_ref, vp_ref, 2)
    z = z_ref[0].astype(F32)

    bd = bd_ref[0]
    lane = lax.broadcasted_iota(jnp.int32, bd.shape, 1)
    beta_all = _sigmoid(bd)
    g_all = -jnp.exp(alog_ref[...]) * _softplus(bd + dtb_ref[...])
    g_all = jnp.where((lane >= DN_HEADS) & (lane < 2 * DN_HEADS), g_all, 0.0)

    c = DN_CHUNK
    row = lax.broadcasted_iota(jnp.int32, (c, c), 0)
    col = lax.broadcasted_iota(jnp.int32, (c, c), 1)
    tri = col <= row
    strict = col < row
    l_incl = tri.astype(F32)
    eye = (col == row).astype(F32)

    for ci in range(DN_T // c):
        r0 = ci * c
        gc = jnp.dot(l_incl, g_all[r0:r0 + c], precision=lax.Precision.HIGHEST,
                     preferred_element_type=F32)
        gct = gc.T
        for h in range(DN_HEADS):
            hs = slice(h * DN_DIM, (h + 1) * DN_DIM)
            qh = q[r0:r0 + c, hs]
            kh = k[r0:r0 + c, hs]
            vh = v[r0:r0 + c, hs]
            qh = qh * lax.rsqrt(jnp.sum(qh * qh, axis=-1, keepdims=True) + NORM_EPS) * (DN_DIM ** -0.5)
            kh = kh * lax.rsqrt(jnp.sum(kh * kh, axis=-1, keepdims=True) + NORM_EPS)
            gcol = gc[:, DN_HEADS + h:DN_HEADS + h + 1]
            grow = gct[DN_HEADS + h:DN_HEADS + h + 1, :]
            bcol = beta_all[r0:r0 + c, h:h + 1]
            glast = gcol[c - 1:c, :]
            decay = jnp.exp(jnp.where(tri, gcol - grow, -1e30))
            kb = kh.astype(BF16)
            kk = _dot_nt(kb, kb)
            a = jnp.where(strict, bcol * kk * decay, 0.0)
            t_inv = eye - a
            x = a
            for _ in range(5):
                x = _dot(x, x)
                t_inv = t_inv + _dot(t_inv, x)
            egc = jnp.exp(gcol)
            rhs = jnp.concatenate([vh * bcol, kh * (bcol * egc)], axis=-1)
            sol = _dot(t_inv, rhs)
            u_val = sol[:, :DN_DIM]
            w_kd = sol[:, DN_DIM:]
            qk = jnp.where(tri, _dot_nt(qh.astype(BF16), kb) * decay, 0.0)
            q_dec = qh * egc
            k_tail = kh * jnp.exp(glast - gcol)
            state = state_ref[h]
            v_new = u_val - _dot(w_kd, state)
            o = _dot(q_dec, state) + _dot(qk, v_new)
            state_ref[h] = state * jnp.exp(glast) + _dot_tn(k_tail, v_new)
            o = _rms(o, ng_ref[...])
            zh = z[r0:r0 + c, hs]
            o_ref[0, r0:r0 + c, hs] = (o * (zh * _sigmoid(zh))).astype(BF16)


def _deltanet(p_main, bd, conv_w, alog_l, dtb_l, ng):
    bsz, s, _ = p_main.shape
    w = BRANCH_W
    hb = DN_T // HALO

    def cur(col):
        return pl.BlockSpec((1, DN_T, w), lambda b, i: (b, i, col // w))

    def prev(col):
        return pl.BlockSpec((1, HALO, w), lambda b, i: (b, jnp.maximum(i * hb - 1, 0), col // w))

    return pl.pallas_call(
        _dn_kernel,
        grid=(bsz, s // DN_T),
        in_specs=[
            cur(COL_BQ), cur(COL_BK), cur(COL_BV), cur(COL_BZ),
            prev(COL_BQ), prev(COL_BK), prev(COL_BV),
            pl.BlockSpec((1, DN_T, LANES), lambda b, i: (b, i, 0)),
            _const_spec((CONV_W, 3 * w)),
            _const_spec((1, LANES)),
            _const_spec((1, LANES)),
            _const_spec((1, DN_DIM)),
        ],
        out_specs=pl.BlockSpec((1, DN_T, w), lambda b, i: (b, i, 0)),
        out_shape=jax.ShapeDtypeStruct((bsz, s, w), BF16),
        scratch_shapes=[pltpu.VMEM((DN_HEADS, DN_DIM, DN_DIM), F32)],
        compiler_params=_cparams(("parallel", "arbitrary"), 32),
        name="deltanet",
    )(p_main, p_main, p_main, p_main, p_main, p_main, p_main, bd, conv_w, alog_l, dtb_l, ng)


SB_T = 128


def _sb_kernel(q_ref, kt_ref, v_ref, o_ref):
    i = pl.program_id(2)
    t = SB_T
    q2 = q_ref[0]
    row = lax.broadcasted_iota(jnp.int32, (t, t), 0)
    col = lax.broadcasted_iota(jnp.int32, (t, t), 1)
    first_half = col < SB_DIM
    q_heads = (jnp.where(first_half, q2, jnp.zeros_like(q2)),
               jnp.where(first_half, jnp.zeros_like(q2), q2))
    after = (row > col).astype(BF16)
    causal = col < row

    def tile(qh, kt, vv, carry, acc, mask):
        z = _dot(qh, kt)
        log_keep = -_softplus(z)
        if mask is not None:
            log_keep = jnp.where(mask, log_keep, 0.0)
        hi = log_keep.astype(BF16)
        lo = (log_keep - hi.astype(F32)).astype(BF16)
        later = _dot(hi, after) + _dot(lo, after)
        w = jnp.exp(z + log_keep + later + carry)
        if mask is not None:
            w = jnp.where(mask, w, 0.0)
        acc = acc + _dot(w.astype(BF16), vv)
        carry = carry + jnp.sum(log_keep, axis=1, keepdims=True)
        return carry, acc

    zc = jnp.zeros((t, 1), F32)
    za = jnp.zeros((t, LANES), F32)
    kt = kt_ref[0, 0, i]
    vv = v_ref[0, pl.ds(pl.multiple_of(i * t, t), t), :]
    ca, aa = tile(q_heads[0], kt, vv, zc, za, causal)
    cb, ab = tile(q_heads[1], kt, vv, zc, za, causal)

    def body(n, st):
        ca, aa, cb, ab = st
        j = i - 1 - n
        kt = kt_ref[0, 0, j]
        vv = v_ref[0, pl.ds(pl.multiple_of(j * t, t), t), :]
        ca, aa = tile(q_heads[0], kt, vv, ca, aa, None)
        cb, ab = tile(q_heads[1], kt, vv, cb, ab, None)
        return ca, aa, cb, ab

    ca, aa, cb, ab = lax.fori_loop(0, i, body, (ca, aa, cb, ab))
    o_ref[0] = jnp.where(first_half, aa, ab).astype(BF16)


def _sb_attn(p_main, kt):
    bsz, s, _ = p_main.shape
    n_hp = SB_HEADS // 2
    nb = s // SB_T
    return pl.pallas_call(
        _sb_kernel,
        grid=(bsz, n_hp, nb),
        in_specs=[
            pl.BlockSpec((1, SB_T, LANES), lambda b, h, i: (b, i, COL_CQ // LANES + h)),
            pl.BlockSpec((1, 1, nb, LANES, LANES), lambda b, h, i: (b, h, 0, 0, 0)),
            pl.BlockSpec((1, s, LANES), lambda b, h, i: (b, 0, COL_CV // LANES + h)),
        ],
        out_specs=pl.BlockSpec((1, SB_T, LANES), lambda b, h, i: (b, i, h)),
        out_shape=jax.ShapeDtypeStruct((bsz, s, BRANCH_W), BF16),
        compiler_params=_cparams(("parallel", "parallel", "arbitrary"), 32),
        name="sb_attn",
    )(p_main, kt, p_main)


MG_TM = 512


def _merge_kernel(x_ref, ga_ref, gb_ref, gc_ref, ya_ref, yb_ref, yc_ref, wb_ref, wo_ref, ng_ref, o_ref):
    m = None
    for n, (g_ref, y_ref) in enumerate(((ga_ref, ya_ref), (gb_ref, yb_ref), (gc_ref, yc_ref))):
        proj = _dot(y_ref[0], wb_ref[n])
        gate = _sigmoid(g_ref[0].astype(F32))
        m = gate * proj if m is None else m + gate * proj
    mixed = _dot(m.astype(BF16), wo_ref[...])
    o_ref[0] = x_ref[0] + _rms(mixed, ng_ref[...])


def _merge(x, p_main, ya, yb, yc, wb, wo, ng):
    bsz, s, d = x.shape
    w = BRANCH_W
    y_spec = pl.BlockSpec((1, MG_TM, w), lambda b, i: (b, i, 0))

    def gate_spec(n):
        return pl.BlockSpec((1, MG_TM, d), lambda b, i: (b, i, COL_GATE // d + n))

    return pl.pallas_call(
        _merge_kernel,
        grid=(bsz, s // MG_TM),
        in_specs=[
            pl.BlockSpec((1, MG_TM, d), lambda b, i: (b, i, 0)),
            gate_spec(0), gate_spec(1), gate_spec(2),
            y_spec, y_spec, y_spec,
            _const_spec((N_BRANCH, w, d)),
            _const_spec((d, d)),
            _const_spec((1, d)),
        ],
        out_specs=pl.BlockSpec((1, MG_TM, d), lambda b, i: (b, i, 0)),
        out_shape=jax.ShapeDtypeStruct((bsz, s, d), F32),
        compiler_params=_cparams(("parallel", "parallel"), 48),
        name="merge",
    )(x, p_main, p_main, p_main, ya, yb, yc, wb, wo, ng)


FF_TM = 512
FF_CW = 1024


def _ffn_kernel(x_ref, g1_ref, w1_ref, w2_ref, g2_ref, o_ref):
    x = x_ref[0]
    hb = _rms(x, g1_ref[...]).astype(BF16)
    dff = w1_ref.shape[1]
    f = None
    for c in range(0, dff, FF_CW):
        a = jnp.maximum(_dot(hb, w1_ref[:, c:c + FF_CW]), 0.0)
        part = _dot((a * a).astype(BF16), w2_ref[c:c + FF_CW, :])
        f = part if f is None else f + part
    o_ref[0] = x + _rms(f, g2_ref[...])


def _ffn(x, g1, w1, w2, g2):
    bsz, s, d = x.shape
    dff = w1.shape[1]
    return pl.pallas_call(
        _ffn_kernel,
        grid=(bsz, s // FF_TM),
        in_specs=[
            pl.BlockSpec((1, FF_TM, d), lambda b, i: (b, i, 0)),
            _const_spec((1, d)),
            _const_spec((d, dff)),
            _const_spec((dff, d)),
            _const_spec((1, d)),
        ],
        out_specs=pl.BlockSpec((1, FF_TM, d), lambda b, i: (b, i, 0)),
        out_shape=jax.ShapeDtypeStruct((bsz, s, d), F32),
        compiler_params=_cparams(("parallel", "parallel"), 56),
        name="ffn",
    )(x, g1, w1, w2, g2)


def _prep_in_proj(w, b):
    w_sz = BRANCH_W
    sizes = (w_sz,) * 6 + (DN_HEADS, DN_HEADS) + (w_sz,) * 3
    offs = [0]
    for sz in sizes:
        offs.append(offs[-1] + sz)

    def cols(t, n):
        return t[..., offs[n]:offs[n + 1]]

    def rest(t):
        return t[..., offs[-1]:]

    scale = SB_DIM ** -0.5
    order = lambda t: jnp.concatenate(
        [cols(t, 0), cols(t, 1), cols(t, 2), cols(t, 3), cols(t, 4), cols(t, 5),
         cols(t, 8) * scale, cols(t, 10), rest(t)], axis=-1)
    w_main = order(w).astype(BF16)
    b_main = order(b)[None, :]
    pad = LANES - 2 * DN_HEADS
    w_bd = jnp.pad(jnp.concatenate([cols(w, 6), cols(w, 7)], axis=-1), ((0, 0), (0, pad))).astype(BF16)
    b_bd = jnp.pad(jnp.concatenate([cols(b, 6), cols(b, 7)], axis=-1), (0, pad))[None, :]
    w_kt = cols(w, 9).T.astype(BF16)
    b_k = cols(b, 9)[:, None]
    return w_main, b_main, w_bd, b_bd, w_kt, b_k


def _lane_row(vals, start):
    return jnp.zeros((1, LANES), F32).at[0, start:start + vals.shape[0]].set(vals.astype(F32))


def kernel(x, norm_g, w_in, b_in, sgu_ln_g, sgu_ln_b, w_spatial, b_spatial, conv_w, a_log, dt_bias,
           dn_norm_g, w_branch, w_out, w_ff1, w_ff2):
    depth = norm_g.shape[0]
    x = x.astype(F32)
    for l in range(depth):
        w_main, b_main, w_bd, b_bd, w_kt, b_k = _prep_in_proj(w_in[l], b_in[l])
        p_main, bd, kt = _in_proj(x, norm_g[l, 0][None, :], w_main, b_main, w_bd, b_bd, w_kt, b_k)

        bs_full = jnp.repeat(b_spatial[l].T, BRANCH_W // GM_GROUPS, axis=1)
        y_a = _sgu(p_main, w_spatial[l], bs_full, sgu_ln_g[l][None, :], sgu_ln_b[l][None, :])

        y_b = _deltanet(p_main, bd, conv_w[l], _lane_row(a_log[l], DN_HEADS),
                        _lane_row(dt_bias[l], DN_HEADS), dn_norm_g[l][None, :])

        y_c = _sb_attn(p_main, kt)

        x = _merge(x, p_main, y_a, y_b, y_c, w_branch[l].astype(BF16), w_out[l].astype(BF16),
                   norm_g[l, 1][None, :])
        x = _ffn(x, norm_g[l, 2][None, :], w_ff1[l].astype(BF16), w_ff2[l].astype(BF16),
                 norm_g[l, 3][None, :])
    return x
```

```python
import functools

import jax
import jax.numpy as jnp
from jax import lax
from jax.experimental import pallas as pl
from jax.experimental.pallas import tpu as pltpu

F32 = jnp.float32
BF16 = jnp.bfloat16
NORM_EPS = 1e-6

LANES = 128
MIB = 1024 * 1024

BRANCH_W = 512
GM_CHUNK = 128
GM_GROUPS = 8
DN_HEADS = 4
DN_DIM = 128
DN_CHUNK = 64
CONV_W = 4
SB_HEADS = 8
SB_DIM = 64
N_BRANCH = 3

COL_AU, COL_AV, COL_BQ, COL_BK, COL_BV, COL_BZ, COL_CQ, COL_CV, COL_GATE = (
    0, 512, 1024, 1536, 2048, 2560, 3072, 3584, 4096)


SB_T = 256
SB_SKIP = 100.0


def _cparams(sem, vmem_mib):
    return pltpu.CompilerParams(dimension_semantics=sem, vmem_limit_bytes=vmem_mib * MIB)


def _const_spec(shape):
    nd = len(shape)
    return pl.BlockSpec(shape, lambda *_: (0,) * nd, pipeline_mode=pl.Buffered(1))


def _rms(x, g):
    return x * lax.rsqrt(jnp.mean(x * x, axis=-1, keepdims=True) + NORM_EPS) * g


def _sigmoid(x):
    return 1.0 / (1.0 + jnp.exp(-x))


def _softplus(x):
    return jnp.maximum(x, 0.0) + jnp.log(1.0 + jnp.exp(-jnp.abs(x)))


def _gelu(x):
    return 0.5 * x * (1.0 + lax.erf(x * (2.0 ** -0.5)))


def _dot(a, b):
    return jnp.dot(a, b, preferred_element_type=F32)


def _dot_nt(a, b):
    return lax.dot_general(a, b, (((1,), (1,)), ((), ())), preferred_element_type=F32)


def _dot_tn(a, b):
    return lax.dot_general(a, b, (((0,), (0,)), ((), ())), preferred_element_type=F32)


IN_TM = 512
IN_CW = 1024


def _in_proj_kernel(x_ref, g_ref, w_ref, b_ref, wbd_ref, bbd_ref, wkt_ref, bk_ref,
                    p_ref, bd_ref, kt_ref):
    hb = _rms(x_ref[0], g_ref[...]).astype(BF16)
    p_cols = w_ref.shape[1]
    for c in range(0, p_cols, IN_CW):
        acc = _dot(hb, w_ref[:, c:c + IN_CW]) + b_ref[:, c:c + IN_CW]
        p_ref[0, :, c:c + IN_CW] = acc.astype(BF16)
    bd_ref[0] = _dot(hb, wbd_ref[...]) + bbd_ref[...]
    kt = _dot_nt(wkt_ref[...], hb) + bk_ref[...]
    for hp in range(SB_HEADS // 2):
        for jb in range(IN_TM // SB_T):
            kt_ref[0, hp, jb] = kt[hp * LANES:(hp + 1) * LANES,
                                   jb * SB_T:(jb + 1) * SB_T].astype(BF16)


def _in_proj(x, g, w_main, b_main, w_bd, b_bd, w_kt, b_k):
    bsz, s, d = x.shape
    pm = w_main.shape[1]
    n_hp = SB_HEADS // 2
    grid = (bsz, s // IN_TM)
    return pl.pallas_call(
        _in_proj_kernel,
        grid=grid,
        in_specs=[
            pl.BlockSpec((1, IN_TM, d), lambda b, i: (b, i, 0)),
            _const_spec((1, d)),
            _const_spec((d, pm)),
            _const_spec((1, pm)),
            _const_spec((d, LANES)),
            _const_spec((1, LANES)),
            _const_spec((BRANCH_W, d)),
            _const_spec((BRANCH_W, 1)),
        ],
        out_specs=[
            pl.BlockSpec((1, IN_TM, pm), lambda b, i: (b, i, 0)),
            pl.BlockSpec((1, IN_TM, LANES), lambda b, i: (b, i, 0)),
            pl.BlockSpec((1, n_hp, IN_TM // SB_T, LANES, SB_T), lambda b, i: (b, 0, i, 0, 0)),
        ],
        out_shape=[
            jax.ShapeDtypeStruct((bsz, s, pm), BF16),
            jax.ShapeDtypeStruct((bsz, s, LANES), F32),
            jax.ShapeDtypeStruct((bsz, n_hp, s // SB_T, LANES, SB_T), BF16),
        ],
        compiler_params=_cparams(("parallel", "parallel"), 56),
        name="in_proj",
    )(x, g, w_main, b_main, w_bd, b_bd, w_kt, b_k)


SGU_T = 256


def _sgu_kernel(u_ref, v_ref, ws_ref, bs_ref, lng_ref, lnb_ref, o_ref):
    u = _gelu(u_ref[0].astype(F32))
    v = _gelu(v_ref[0].astype(F32))
    vc = v - jnp.mean(v, axis=-1, keepdims=True)
    v = vc * lax.rsqrt(jnp.mean(vc * vc, axis=-1, keepdims=True) + NORM_EPS)
    v = (v * lng_ref[...] + lnb_ref[...]).astype(BF16)
    row = lax.broadcasted_iota(jnp.int32, (GM_CHUNK, GM_CHUNK), 0)
    col = lax.broadcasted_iota(jnp.int32, (GM_CHUNK, GM_CHUNK), 1)
    causal = col <= row
    first_half = col < (LANES // 2)
    ws = [jnp.where(causal, ws_ref[g], 0.0).astype(BF16) for g in range(GM_GROUPS)]
    for c in range(SGU_T // GM_CHUNK):
        r0 = c * GM_CHUNK
        for p in range(GM_GROUPS // 2):
            v2 = v[r0:r0 + GM_CHUNK, p * LANES:(p + 1) * LANES]
            mixed = jnp.where(first_half, _dot(ws[2 * p], v2), _dot(ws[2 * p + 1], v2))
            mixed = mixed + bs_ref[:, p * LANES:(p + 1) * LANES]
            y = u[r0:r0 + GM_CHUNK, p * LANES:(p + 1) * LANES] * mixed
            o_ref[0, r0:r0 + GM_CHUNK, p * LANES:(p + 1) * LANES] = y.astype(BF16)


def _sgu(p_main, ws, bs_full, lng, lnb):
    bsz, s, _ = p_main.shape
    w = BRANCH_W
    return pl.pallas_call(
        _sgu_kernel,
        grid=(bsz, s // SGU_T),
        in_specs=[
            pl.BlockSpec((1, SGU_T, w), lambda b, i: (b, i, COL_AU // w)),
            pl.BlockSpec((1, SGU_T, w), lambda b, i: (b, i, COL_AV // w)),
            _const_spec((GM_GROUPS, GM_CHUNK, GM_CHUNK)),
            _const_spec((GM_CHUNK, w)),
            _const_spec((1, w)),
            _const_spec((1, w)),
        ],
        out_specs=pl.BlockSpec((1, SGU_T, w), lambda b, i: (b, i, 0)),
        out_shape=jax.ShapeDtypeStruct((bsz, s, w), BF16),
        compiler_params=_cparams(("parallel", "parallel"), 32),
        name="sgu",
    )(p_main, p_main, ws, bs_full, lng, lnb)


DN_T = 256
HALO = 8


def _dn_kernel(q_ref, k_ref, v_ref, z_ref, qp_ref, kp_ref, vp_ref, bd_ref,
               cw_ref, alog_ref, dtb_ref, ng_ref, o_ref, state_ref):
    sb = pl.program_id(1)

    @pl.when(sb == 0)
    def _():
        state_ref[...] = jnp.zeros_like(state_ref)

    not_first = (sb > 0).astype(F32)

    def conv_silu(x_ref, xp_ref, part):
        x = x_ref[0].astype(F32)
        xp = xp_ref[0].astype(F32) * not_first
        xc = jnp.concatenate([xp, x], axis=0)
        y = None
        for i in range(CONV_W):
            off = HALO - (CONV_W - 1) + i
            term = cw_ref[i:i + 1, part * BRANCH_W:(part + 1) * BRANCH_W] * xc[off:off + DN_T]
            y = term if y is None else y + term
        return y * _sigmoid(y)

    q = conv_silu(q_ref, qp_ref, 0)
    k = conv_silu(k_ref, kp_ref, 1)
    v = conv_silu(v_ref, vp_ref, 2)
    z = z_ref[0].astype(F32)

    bd = bd_ref[0]
    lane = lax.broadcasted_iota(jnp.int32, bd.shape, 1)
    beta_all = _sigmoid(bd)
    g_all = -jnp.exp(alog_ref[...]) * _softplus(bd + dtb_ref[...])
    g_all = jnp.where((lane >= DN_HEADS) & (lane < 2 * DN_HEADS), g_all, 0.0)

    t = DN_T
    c = DN_CHUNK
    nc = t // c
    row = lax.broadcasted_iota(jnp.int32, (t, t), 0)
    col = lax.broadcasted_iota(jnp.int32, (t, t), 1)
    same_chunk = (row // c) == (col // c)
    tri = same_chunk & (col <= row)
    strict = same_chunk & (col < row)
    eye = (col == row).astype(F32)
    col_k = lax.broadcasted_iota(jnp.int32, (DN_DIM, t), 1) // c

    gc = jnp.dot(tri.astype(F32), g_all, precision=lax.Precision.HIGHEST,
                 preferred_element_type=F32)
    gct = gc.T

    u_l, w_l, qd_l, qk_l, ktt_l, gl_l = [], [], [], [], [], []
    for h in range(DN_HEADS):
        hs = slice(h * DN_DIM, (h + 1) * DN_DIM)
        qh, kh, vh = q[:, hs], k[:, hs], v[:, hs]
        qh = qh * lax.rsqrt(jnp.sum(qh * qh, axis=-1, keepdims=True) + NORM_EPS) * (DN_DIM ** -0.5)
        kh = kh * lax.rsqrt(jnp.sum(kh * kh, axis=-1, keepdims=True) + NORM_EPS)
        gcol = gc[:, DN_HEADS + h:DN_HEADS + h + 1]
        grow = gct[DN_HEADS + h:DN_HEADS + h + 1, :]
        bcol = beta_all[:, h:h + 1]
        decay = jnp.exp(jnp.where(tri, gcol - grow, -1e30))
        kb = kh.astype(BF16)
        a = jnp.where(strict, bcol * _dot_nt(kb, kb) * decay, 0.0)
        t_inv = eye - a
        x = a
        for _ in range(5):
            xb = x.astype(BF16)
            x = _dot(xb, xb)
            t_inv = t_inv + _dot(t_inv.astype(BF16), x.astype(BF16))
        egc = jnp.exp(gcol)
        rhs = jnp.concatenate([vh * bcol, kh * (bcol * egc)], axis=-1)
        sol = _dot(t_inv.astype(BF16), rhs.astype(BF16))
        u_l.append(sol[:, :DN_DIM])
        w_l.append(sol[:, DN_DIM:])
        qk_l.append(jnp.where(tri, _dot_nt(qh.astype(BF16), kb) * decay, 0.0).astype(BF16))
        qd_l.append(qh * egc)
        glast = [gcol[ci * c + c - 1:ci * c + c, :] for ci in range(nc)]
        gl_l.append(glast)
        glast_rows = jnp.concatenate([jnp.broadcast_to(g, (c, 1)) for g in glast], axis=0)
        k_tail = kh * jnp.exp(glast_rows - gcol)
        ktt_l.append(k_tail.T.astype(BF16))

    for ci in range(nc):
        rs = slice(ci * c, (ci + 1) * c)
        for h in range(DN_HEADS):
            hs = slice(h * DN_DIM, (h + 1) * DN_DIM)
            state = state_ref[h]
            r1 = _dot(jnp.concatenate([w_l[h][rs], qd_l[h][rs]], axis=0).astype(BF16),
                      state.astype(BF16))
            v_new = u_l[h][rs] - r1[:c]
            pieces = []
            if ci > 0:
                pieces.append(jnp.zeros((ci * c, DN_DIM), F32))
            pieces.append(v_new)
            if ci < nc - 1:
                pieces.append(jnp.zeros(((nc - 1 - ci) * c, DN_DIM), F32))
            v_pad = jnp.concatenate(pieces, axis=0).astype(BF16)
            ktt_c = jnp.where(col_k == ci, ktt_l[h], jnp.zeros_like(ktt_l[h]))
            r2 = _dot(jnp.concatenate([qk_l[h][rs, :], ktt_c], axis=0), v_pad)
            o = r1[c:] + r2[:c]
            state_ref[h] = state * jnp.exp(gl_l[h][ci]) + r2[c:]
            o = _rms(o, ng_ref[...])
            zh = z[rs, hs]
            o_ref[0, rs, hs] = (o * (zh * _sigmoid(zh))).astype(BF16)


def _deltanet(p_main, bd, conv_w, alog_l, dtb_l, ng):
    bsz, s, _ = p_main.shape
    w = BRANCH_W
    hb = DN_T // HALO

    def cur(col):
        return pl.BlockSpec((1, DN_T, w), lambda b, i: (b, i, col // w))

    def prev(col):
        return pl.BlockSpec((1, HALO, w), lambda b, i: (b, jnp.maximum(i * hb - 1, 0), col // w))

    return pl.pallas_call(
        _dn_kernel,
        grid=(bsz, s // DN_T),
        in_specs=[
            cur(COL_BQ), cur(COL_BK), cur(COL_BV), cur(COL_BZ),
            prev(COL_BQ), prev(COL_BK), prev(COL_BV),
            pl.BlockSpec((1, DN_T, LANES), lambda b, i: (b, i, 0)),
            _const_spec((CONV_W, 3 * w)),
            _const_spec((1, LANES)),
            _const_spec((1, LANES)),
            _const_spec((1, DN_DIM)),
        ],
        out_specs=pl.BlockSpec((1, DN_T, w), lambda b, i: (b, i, 0)),
        out_shape=jax.ShapeDtypeStruct((bsz, s, w), BF16),
        scratch_shapes=[pltpu.VMEM((DN_HEADS, DN_DIM, DN_DIM), F32)],
        compiler_params=_cparams(("parallel", "arbitrary"), 32),
        name="deltanet",
    )(p_main, p_main, p_main, p_main, p_main, p_main, p_main, bd, conv_w, alog_l, dtb_l, ng)


def _sb_kernel(q_ref, kt_ref, v_ref, o_ref):
    i = pl.program_id(2)
    t = SB_T
    q2 = q_ref[0]
    lane = lax.broadcasted_iota(jnp.int32, (t, LANES), 1)
    head0 = lane < SB_DIM
    zq = jnp.zeros_like(q2)
    qs = jnp.concatenate([jnp.where(head0, q2, zq), jnp.where(head0, zq, q2)], axis=0)
    row = lax.broadcasted_iota(jnp.int32, (t, t), 0)
    col = lax.broadcasted_iota(jnp.int32, (t, t), 1)
    after = (row > col).astype(BF16)
    causal = jnp.concatenate([col < row, col < row], axis=0)

    def tile(j, carry, acc, mask):
        kt = kt_ref[0, 0, j]
        vv = v_ref[0, pl.ds(pl.multiple_of(j * t, t), t), :]
        z = _dot(qs, kt)
        log_keep = -_softplus(z)
        if mask is not None:
            log_keep = jnp.where(mask, log_keep, 0.0)
        hi = log_keep.astype(BF16)
        lo = (log_keep - hi.astype(F32)).astype(BF16)
        both = _dot(jnp.concatenate([hi, lo], axis=0), after)
        later = both[:2 * t] + both[2 * t:]
        w = jnp.exp(z + log_keep + later + carry)
        if mask is not None:
            w = jnp.where(mask, w, 0.0)
        acc = acc + _dot(w.astype(BF16), vv)
        carry = carry + jnp.sum(log_keep, axis=1, keepdims=True)
        return carry, acc

    carry, acc = tile(i, jnp.zeros((2 * t, 1), F32), jnp.zeros((2 * t, LANES), F32), causal)

    def cond(st):
        j, carry, _ = st
        return jnp.logical_and(j >= 0, jnp.max(carry) > -SB_SKIP)

    def body(st):
        j, carry, acc = st
        carry, acc = tile(j, carry, acc, None)
        return j - 1, carry, acc

    _, _, acc = lax.while_loop(cond, body, (i - 1, carry, acc))
    o_ref[0] = jnp.where(head0, acc[:t], acc[t:]).astype(BF16)


def _sb_attn(p_main, kt):
    bsz, s, _ = p_main.shape
    n_hp = SB_HEADS // 2
    nb = s // SB_T
    return pl.pallas_call(
        _sb_kernel,
        grid=(bsz, n_hp, nb),
        in_specs=[
            pl.BlockSpec((1, SB_T, LANES), lambda b, h, i: (b, i, COL_CQ // LANES + h)),
            pl.BlockSpec((1, 1, nb, LANES, SB_T), lambda b, h, i: (b, h, 0, 0, 0)),
            pl.BlockSpec((1, s, LANES), lambda b, h, i: (b, 0, COL_CV // LANES + h)),
        ],
        out_specs=pl.BlockSpec((1, SB_T, LANES), lambda b, h, i: (b, i, h)),
        out_shape=jax.ShapeDtypeStruct((bsz, s, BRANCH_W), BF16),
        compiler_params=_cparams(("parallel", "parallel", "arbitrary"), 32),
        name="sb_attn",
    )(p_main, kt, p_main)


MG_TM = 512


def _merge_kernel(x_ref, ga_ref, gb_ref, gc_ref, ya_ref, yb_ref, yc_ref, wb_ref, wo_ref, ng_ref, o_ref):
    m = None
    for n, (g_ref, y_ref) in enumerate(((ga_ref, ya_ref), (gb_ref, yb_ref), (gc_ref, yc_ref))):
        proj = _dot(y_ref[0], wb_ref[n])
        gate = _sigmoid(g_ref[0].astype(F32))
        m = gate * proj if m is None else m + gate * proj
    mixed = _dot(m.astype(BF16), wo_ref[...])
    o_ref[0] = x_ref[0] + _rms(mixed, ng_ref[...])


def _merge(x, p_main, ya, yb, yc, wb, wo, ng):
    bsz, s, d = x.shape
    w = BRANCH_W
    y_spec = pl.BlockSpec((1, MG_TM, w), lambda b, i: (b, i, 0))

    def gate_spec(n):
        return pl.BlockSpec((1, MG_TM, d), lambda b, i: (b, i, COL_GATE // d + n))

    return pl.pallas_call(
        _merge_kernel,
        grid=(bsz, s // MG_TM),
        in_specs=[
            pl.BlockSpec((1, MG_TM, d), lambda b, i: (b, i, 0)),
            gate_spec(0), gate_spec(1), gate_spec(2),
            y_spec, y_spec, y_spec,
            _const_spec((N_BRANCH, w, d)),
            _const_spec((d, d)),
            _const_spec((1, d)),
        ],
        out_specs=pl.BlockSpec((1, MG_TM, d), lambda b, i: (b, i, 0)),
        out_shape=jax.ShapeDtypeStruct((bsz, s, d), F32),
        compiler_params=_cparams(("parallel", "parallel"), 48),
        name="merge",
    )(x, p_main, p_main, p_main, ya, yb, yc, wb, wo, ng)


FF_TM = 512
FF_CW = 1024


def _ffn_kernel(x_ref, g1_ref, w1_ref, w2_ref, g2_ref, o_ref):
    x = x_ref[0]
    hb = _rms(x, g1_ref[...]).astype(BF16)
    dff = w1_ref.shape[1]
    f = None
    for c in range(0, dff, FF_CW):
        a = jnp.maximum(_dot(hb, w1_ref[:, c:c + FF_CW]), 0.0)
        part = _dot((a * a).astype(BF16), w2_ref[c:c + FF_CW, :])
        f = part if f is None else f + part
    o_ref[0] = x + _rms(f, g2_ref[...])


def _ffn(x, g1, w1, w2, g2):
    bsz, s, d = x.shape
    dff = w1.shape[1]
    return pl.pallas_call(
        _ffn_kernel,
        grid=(bsz, s // FF_TM),
        in_specs=[
            pl.BlockSpec((1, FF_TM, d), lambda b, i: (b, i, 0)),
            _const_spec((1, d)),
            _const_spec((d, dff)),
            _const_spec((dff, d)),
            _const_spec((1, d)),
        ],
        out_specs=pl.BlockSpec((1, FF_TM, d), lambda b, i: (b, i, 0)),
        out_shape=jax.ShapeDtypeStruct((bsz, s, d), F32),
        compiler_params=_cparams(("parallel", "parallel"), 56),
        name="ffn",
    )(x, g1, w1, w2, g2)


def _prep_in_proj(w, b):
    w_sz = BRANCH_W
    sizes = (w_sz,) * 6 + (DN_HEADS, DN_HEADS) + (w_sz,) * 3
    offs = [0]
    for sz in sizes:
        offs.append(offs[-1] + sz)

    def cols(t, n):
        return t[..., offs[n]:offs[n + 1]]

    def rest(t):
        return t[..., offs[-1]:]

    scale = SB_DIM ** -0.5
    order = lambda t: jnp.concatenate(
        [cols(t, 0), cols(t, 1), cols(t, 2), cols(t, 3), cols(t, 4), cols(t, 5),
         cols(t, 8) * scale, cols(t, 10), rest(t)], axis=-1)
    w_main = order(w).astype(BF16)
    b_main = order(b)[None, :]
    pad = LANES - 2 * DN_HEADS
    w_bd = jnp.pad(jnp.concatenate([cols(w, 6), cols(w, 7)], axis=-1), ((0, 0), (0, pad))).astype(BF16)
    b_bd = jnp.pad(jnp.concatenate([cols(b, 6), cols(b, 7)], axis=-1), (0, pad))[None, :]
    w_kt = cols(w, 9).T.astype(BF16)
    b_k = cols(b, 9)[:, None]
    return w_main, b_main, w_bd, b_bd, w_kt, b_k


def _lane_row(vals, start):
    return jnp.zeros((1, LANES), F32).at[0, start:start + vals.shape[0]].set(vals.astype(F32))


def kernel(x, norm_g, w_in, b_in, sgu_ln_g, sgu_ln_b, w_spatial, b_spatial, conv_w, a_log, dt_bias,
           dn_norm_g, w_branch, w_out, w_ff1, w_ff2):
    depth = norm_g.shape[0]
    x = x.astype(F32)
    for l in range(depth):
        w_main, b_main, w_bd, b_bd, w_kt, b_k = _prep_in_proj(w_in[l], b_in[l])
        p_main, bd, kt = _in_proj(x, norm_g[l, 0][None, :], w_main, b_main, w_bd, b_bd, w_kt, b_k)

        bs_full = jnp.repeat(b_spatial[l].T, BRANCH_W // GM_GROUPS, axis=1)
        y_a = _sgu(p_main, w_spatial[l], bs_full, sgu_ln_g[l][None, :], sgu_ln_b[l][None, :])

        y_b = _deltanet(p_main, bd, conv_w[l], _lane_row(a_log[l], DN_HEADS),
                        _lane_row(dt_bias[l], DN_HEADS), dn_norm_g[l][None, :])

        y_c = _sb_attn(p_main, kt)

        x = _merge(x, p_main, y_a, y_b, y_c, w_branch[l].astype(BF16), w_out[l].astype(BF16),
                   norm_g[l, 1][None, :])
        x = _ffn(x, norm_g[l, 2][None, :], w_ff1[l].astype(BF16), w_ff2[l].astype(BF16),
                 norm_g[l, 3][None, :])
    return x
```

```python
import functools

import jax
import jax.numpy as jnp
from jax import lax
from jax.experimental import pallas as pl
from jax.experimental.pallas import tpu as pltpu

F32 = jnp.float32
BF16 = jnp.bfloat16
NORM_EPS = 1e-6

LANES = 128
MIB = 1024 * 1024

BRANCH_W = 512
GM_CHUNK = 128
GM_GROUPS = 8
DN_HEADS = 4
DN_DIM = 128
DN_CHUNK = 64
CONV_W = 4
SB_HEADS = 8
SB_DIM = 64
N_BRANCH = 3

COL_AU, COL_AV, COL_BQ, COL_BK, COL_BV, COL_BZ, COL_CQ, COL_CV, COL_GATE = (
    0, 512, 1024, 1536, 2048, 2560, 3072, 3584, 4096)


SB_T = 256
LOG2E = 1.4426950408889634
SB_SKIP = 100.0 * LOG2E
SB_GROUP = 2


def _cparams(sem, vmem_mib):
    return pltpu.CompilerParams(dimension_semantics=sem, vmem_limit_bytes=vmem_mib * MIB)


def _const_spec(shape):
    nd = len(shape)
    return pl.BlockSpec(shape, lambda *_: (0,) * nd, pipeline_mode=pl.Buffered(1))


def _rms(x, g):
    return x * lax.rsqrt(jnp.mean(x * x, axis=-1, keepdims=True) + NORM_EPS) * g


def _sigmoid(x):
    return 1.0 / (1.0 + jnp.exp(-x))


def _softplus(x):
    return jnp.maximum(x, 0.0) + jnp.log(1.0 + jnp.exp(-jnp.abs(x)))


def _gelu(x):
    return 0.5 * x * (1.0 + lax.erf(x * (2.0 ** -0.5)))


def _dot(a, b):
    return jnp.dot(a, b, preferred_element_type=F32)


def _dot_nt(a, b):
    return lax.dot_general(a, b, (((1,), (1,)), ((), ())), preferred_element_type=F32)


IN_TM = 512
IN_CW = 1024


def _in_proj_kernel(x_ref, g_ref, w_ref, b_ref, wbd_ref, bbd_ref, wkt_ref, bk_ref,
                    p_ref, bd_ref, kt_ref):
    hb = _rms(x_ref[0], g_ref[...]).astype(BF16)
    p_cols = w_ref.shape[1]
    for c in range(0, p_cols, IN_CW):
        acc = _dot(hb, w_ref[:, c:c + IN_CW]) + b_ref[:, c:c + IN_CW]
        p_ref[0, :, c:c + IN_CW] = acc.astype(BF16)
    bd_ref[0] = _dot(hb, wbd_ref[...]) + bbd_ref[...]
    kt = _dot_nt(wkt_ref[...], hb) + bk_ref[...]
    for hp in range(SB_HEADS // 2):
        for jb in range(IN_TM // SB_T):
            kt_ref[0, hp, jb] = kt[hp * LANES:(hp + 1) * LANES,
                                   jb * SB_T:(jb + 1) * SB_T].astype(BF16)


def _in_proj(x, g, w_main, b_main, w_bd, b_bd, w_kt, b_k):
    bsz, s, d = x.shape
    pm = w_main.shape[1]
    n_hp = SB_HEADS // 2
    grid = (bsz, s // IN_TM)
    return pl.pallas_call(
        _in_proj_kernel,
        grid=grid,
        in_specs=[
            pl.BlockSpec((1, IN_TM, d), lambda b, i: (b, i, 0)),
            _const_spec((1, d)),
            _const_spec((d, pm)),
            _const_spec((1, pm)),
            _const_spec((d, LANES)),
            _const_spec((1, LANES)),
            _const_spec((BRANCH_W, d)),
            _const_spec((BRANCH_W, 1)),
        ],
        out_specs=[
            pl.BlockSpec((1, IN_TM, pm), lambda b, i: (b, i, 0)),
            pl.BlockSpec((1, IN_TM, LANES), lambda b, i: (b, i, 0)),
            pl.BlockSpec((1, n_hp, IN_TM // SB_T, LANES, SB_T), lambda b, i: (b, 0, i, 0, 0)),
        ],
        out_shape=[
            jax.ShapeDtypeStruct((bsz, s, pm), BF16),
            jax.ShapeDtypeStruct((bsz, s, LANES), F32),
            jax.ShapeDtypeStruct((bsz, n_hp, s // SB_T, LANES, SB_T), BF16),
        ],
        compiler_params=_cparams(("parallel", "parallel"), 56),
        name="in_proj",
    )(x, g, w_main, b_main, w_bd, b_bd, w_kt, b_k)


SGU_T = 256


def _sgu_kernel(u_ref, v_ref, ws_ref, bs_ref, lng_ref, lnb_ref, o_ref):
    u = _gelu(u_ref[0].astype(F32))
    v = _gelu(v_ref[0].astype(F32))
    vc = v - jnp.mean(v, axis=-1, keepdims=True)
    v = vc * lax.rsqrt(jnp.mean(vc * vc, axis=-1, keepdims=True) + NORM_EPS)
    v = (v * lng_ref[...] + lnb_ref[...]).astype(BF16)
    row = lax.broadcasted_iota(jnp.int32, (GM_CHUNK, GM_CHUNK), 0)
    col = lax.broadcasted_iota(jnp.int32, (GM_CHUNK, GM_CHUNK), 1)
    causal = col <= row
    first_half = col < (LANES // 2)
    ws = [jnp.where(causal, ws_ref[g], 0.0).astype(BF16) for g in range(GM_GROUPS)]
    for c in range(SGU_T // GM_CHUNK):
        r0 = c * GM_CHUNK
        for p in range(GM_GROUPS // 2):
            v2 = v[r0:r0 + GM_CHUNK, p * LANES:(p + 1) * LANES]
            mixed = jnp.where(first_half, _dot(ws[2 * p], v2), _dot(ws[2 * p + 1], v2))
            mixed = mixed + bs_ref[:, p * LANES:(p + 1) * LANES]
            y = u[r0:r0 + GM_CHUNK, p * LANES:(p + 1) * LANES] * mixed
            o_ref[0, r0:r0 + GM_CHUNK, p * LANES:(p + 1) * LANES] = y.astype(BF16)


def _sgu(p_main, ws, bs_full, lng, lnb):
    bsz, s, _ = p_main.shape
    w = BRANCH_W
    return pl.pallas_call(
        _sgu_kernel,
        grid=(bsz, s // SGU_T),
        in_specs=[
            pl.BlockSpec((1, SGU_T, w), lambda b, i: (b, i, COL_AU // w)),
            pl.BlockSpec((1, SGU_T, w), lambda b, i: (b, i, COL_AV // w)),
            _const_spec((GM_GROUPS, GM_CHUNK, GM_CHUNK)),
            _const_spec((GM_CHUNK, w)),
            _const_spec((1, w)),
            _const_spec((1, w)),
        ],
        out_specs=pl.BlockSpec((1, SGU_T, w), lambda b, i: (b, i, 0)),
        out_shape=jax.ShapeDtypeStruct((bsz, s, w), BF16),
        compiler_params=_cparams(("parallel", "parallel"), 32),
        name="sgu",
    )(p_main, p_main, ws, bs_full, lng, lnb)


DN_T = 256
DN_SUB = 16
HALO = 8


def _dn_kernel(q_ref, k_ref, v_ref, z_ref, qp_ref, kp_ref, vp_ref, bd_ref,
               cw_ref, alog_ref, dtb_ref, ng_ref, o_ref, state_ref):
    sb = pl.program_id(1)

    @pl.when(sb == 0)
    def _():
        state_ref[...] = jnp.zeros_like(state_ref)

    not_first = (sb > 0).astype(F32)

    def conv_silu(x_ref, xp_ref, part):
        x = x_ref[0].astype(F32)
        xp = xp_ref[0].astype(F32) * not_first
        xc = jnp.concatenate([xp, x], axis=0)
        y = None
        for i in range(CONV_W):
            off = HALO - (CONV_W - 1) + i
            term = cw_ref[i:i + 1, part * BRANCH_W:(part + 1) * BRANCH_W] * xc[off:off + DN_T]
            y = term if y is None else y + term
        return y * _sigmoid(y)

    q = conv_silu(q_ref, qp_ref, 0)
    k = conv_silu(k_ref, kp_ref, 1)
    v = conv_silu(v_ref, vp_ref, 2)
    z = z_ref[0].astype(F32)

    bd = bd_ref[0]
    lane = lax.broadcasted_iota(jnp.int32, bd.shape, 1)
    beta_all = _sigmoid(bd)
    g_all = -jnp.exp(alog_ref[...]) * _softplus(bd + dtb_ref[...])
    g_all = jnp.where((lane >= DN_HEADS) & (lane < 2 * DN_HEADS), g_all, 0.0)

    t = DN_T
    c = DN_CHUNK
    nc = t // c
    row = lax.broadcasted_iota(jnp.int32, (t, t), 0)
    col = lax.broadcasted_iota(jnp.int32, (t, t), 1)
    same_chunk = (row // c) == (col // c)
    tri = same_chunk & (col <= row)
    strict = same_chunk & (col < row)
    same_sub = (row // DN_SUB) == (col // DN_SUB)
    eye = (col == row).astype(F32)
    col_k =lax.broadcasted_iota(jnp.int32, (DN_DIM, t), 1) // c

    gc = jnp.dot(tri.astype(F32), g_all, precision=lax.Precision.HIGHEST,
                 preferred_element_type=F32)
    gct = gc.T

    heads = range(DN_HEADS)
    qn, kn, vn, kb = [], [], [], []
    gcol, bcol, egc, decay, glast, k_tt = [], [], [], [], [], []
    for h in heads:
        hs = slice(h * DN_DIM, (h + 1) * DN_DIM)
        qh, kh = q[:, hs], k[:, hs]
        qn.append(qh * lax.rsqrt(jnp.sum(qh * qh, axis=-1, keepdims=True) + NORM_EPS) * (DN_DIM ** -0.5))
        kn.append(kh * lax.rsqrt(jnp.sum(kh * kh, axis=-1, keepdims=True) + NORM_EPS))
        vn.append(v[:, hs])
        kb.append(kn[h].astype(BF16))
        gcol.append(gc[:, DN_HEADS + h:DN_HEADS + h + 1])
        grow = gct[DN_HEADS + h:DN_HEADS + h + 1, :]
        bcol.append(beta_all[:, h:h + 1])
        egc.append(jnp.exp(gcol[h]))
        decay.append(jnp.exp(jnp.where(tri, gcol[h] - grow, -1e30)))
        glast.append([gcol[h][ci * c + c - 1:ci * c + c, :] for ci in range(nc)])
        glast_rows = jnp.concatenate([jnp.broadcast_to(g, (c, 1)) for g in glast[h]], axis=0)
        k_tt.append((kn[h] * jnp.exp(glast_rows - gcol[h])).T.astype(BF16))

    kk = [_dot_nt(kb[h], kb[h]) for h in heads]
    qk = [_dot_nt(qn[h].astype(BF16), kb[h]) for h in heads]
    qk = [jnp.where(tri, qk[h] * decay[h], 0.0).astype(BF16) for h in heads]
    a = [jnp.where(strict, bcol[h] * kk[h] * decay[h], 0.0) for h in heads]
    a_d = [jnp.where(same_sub, a[h], 0.0) for h in heads]
    a_l = [jnp.where(same_sub, 0.0, a[h]).astype(BF16) for h in heads]
    x = [_dot(a_d[h].astype(BF16), a_d[h].astype(BF16)) for h in heads]
    t_d = [eye - a_d[h] for h in heads]
    for _ in range(2):
        xb = [x[h].astype(BF16) for h in heads]
        both = [_dot(jnp.concatenate([xb[h], t_d[h].astype(BF16)], axis=0), xb[h]) for h in heads]
        x = [both[h][:t] for h in heads]
        t_d = [t_d[h] + both[h][t:] for h in heads]
    last = [_dot(t_d[h].astype(BF16), x[h].astype(BF16)) for h in heads]
    t_d = [(t_d[h] + last[h]).astype(BF16) for h in heads]
    rhs = [jnp.concatenate([vn[h] * bcol[h], kn[h] * (bcol[h] * egc[h])], axis=-1) for h in heads]
    sol = [_dot(t_d[h], rhs[h].astype(BF16)) for h in heads]
    for _ in range(c // DN_SUB - 1):
        below = [_dot(a_l[h], sol[h].astype(BF16)) for h in heads]
        sol = [_dot(t_d[h], (rhs[h] - below[h]).astype(BF16)) for h in heads]
    u_l = [sol[h][:, :DN_DIM] for h in heads]
    w_l = [sol[h][:, DN_DIM:] for h in heads]
    qd_l = [qn[h] * egc[h] for h in heads]

    state = [state_ref[h] for h in heads]
    for ci in range(nc):
        rs = slice(ci * c, (ci + 1) * c)
        r1 = [_dot(jnp.concatenate([w_l[h][rs], qd_l[h][rs]], axis=0).astype(BF16), state[h].astype(BF16))
              for h in heads]
        lhs2, v_pad = [], []
        for h in heads:
            pieces = []
            if ci > 0:
                pieces.append(jnp.zeros((ci * c, DN_DIM), F32))
            pieces.append(u_l[h][rs] - r1[h][:c])
            if ci < nc - 1:
                pieces.append(jnp.zeros(((nc - 1 - ci) * c, DN_DIM), F32))
            v_pad.append(jnp.concatenate(pieces, axis=0).astype(BF16))
            ktt_c = jnp.where(col_k == ci, k_tt[h], jnp.zeros_like(k_tt[h]))
            lhs2.append(jnp.concatenate([qk[h][rs, :], ktt_c], axis=0))
        r2 = [_dot(lhs2[h], v_pad[h]) for h in heads]
        for h in heads:
            hs = slice(h * DN_DIM, (h + 1) * DN_DIM)
            state[h] = state[h] * jnp.exp(glast[h][ci]) + r2[h][c:]
            o = _rms(r1[h][c:] + r2[h][:c], ng_ref[...])
            zh = z[rs, hs]
            o_ref[0, rs, hs] = (o * (zh * _sigmoid(zh))).astype(BF16)
    for h in heads:
        state_ref[h] = state[h]


def _deltanet(p_main, bd, conv_w, alog_l, dtb_l, ng):
    bsz, s, _ = p_main.shape
    w = BRANCH_W
    hb = DN_T // HALO

    def cur(col):
        return pl.BlockSpec((1, DN_T, w), lambda b, i: (b, i, col // w))

    def prev(col):
        return pl.BlockSpec((1, HALO, w), lambda b, i: (b, jnp.maximum(i * hb - 1, 0), col // w))

    return pl.pallas_call(
        _dn_kernel,
        grid=(bsz, s // DN_T),
        in_specs=[
            cur(COL_BQ), cur(COL_BK), cur(COL_BV), cur(COL_BZ),
            prev(COL_BQ), prev(COL_BK), prev(COL_BV),
            pl.BlockSpec((1, DN_T, LANES), lambda b, i: (b, i, 0)),
            _const_spec((CONV_W, 3 * w)),
            _const_spec((1, LANES)),
            _const_spec((1, LANES)),
            _const_spec((1, DN_DIM)),
        ],
        out_specs=pl.BlockSpec((1, DN_T, w), lambda b, i: (b, i, 0)),
        out_shape=jax.ShapeDtypeStruct((bsz, s, w), BF16),
        scratch_shapes=[pltpu.VMEM((DN_HEADS, DN_DIM, DN_DIM), F32)],
        compiler_params=_cparams(("parallel", "arbitrary"), 32),
        name="deltanet",
    )(p_main, p_main, p_main, p_main, p_main, p_main, p_main, bd, conv_w, alog_l, dtb_l, ng)


def _sb_kernel(q_ref, kt_ref, v_ref, o_ref):
    i = pl.program_id(2)
    t = SB_T
    pairs = range(SB_GROUP)
    lane = lax.broadcasted_iota(jnp.int32, (t, LANES), 1)
    head0 = lane < SB_DIM
    qs = []
    for p in pairs:
        q2 = q_ref[0, :, p * LANES:(p + 1) * LANES]
        zq = jnp.zeros_like(q2)
        qs.append(jnp.concatenate([jnp.where(head0, q2, zq), jnp.where(head0, zq, q2)], axis=0))
    row = lax.broadcasted_iota(jnp.int32, (t, t), 0)
    col = lax.broadcasted_iota(jnp.int32, (t, t), 1)
    after = (row > col).astype(BF16)
    causal = jnp.concatenate([col < row, col < row], axis=0)

    def tile(j, carry, acc, mask):
        rows = pl.ds(pl.multiple_of(j * t, t), t)
        z = [_dot(qs[p], kt_ref[0, p, j]) for p in pairs]
        log_keep, log_beta, parts = [], [], []
        for p in pairs:
            y = z[p] * LOG2E
            ny = -y
            lk = jnp.minimum(ny, 0.0) - jnp.log2(1.0 + jnp.exp2(jnp.minimum(y, ny)))
            log_beta.append(y + lk)
            if mask is not None:
                lk = jnp.where(mask, lk, 0.0)
            log_keep.append(lk)
            parts.append(lk.astype(BF16))
        later = [_dot(parts[p], after) for p in pairs]
        w = []
        for p in pairs:
            wp = jnp.exp2(log_beta[p] + later[p] + carry[p])
            if mask is not None:
                wp = jnp.where(mask, wp, 0.0)
            w.append(wp.astype(BF16))
        pv = [_dot(w[p], v_ref[0, rows, p * LANES:(p + 1) * LANES]) for p in pairs]
        acc = tuple(acc[p] + pv[p] for p in pairs)
        carry = tuple(carry[p] + jnp.sum(log_keep[p], axis=1, keepdims=True) for p in pairs)
        return carry, acc

    carry, acc = tile(i, (jnp.zeros((2 * t, 1), F32),) * SB_GROUP,
                      (jnp.zeros((2 * t, LANES), F32),) * SB_GROUP, causal)

    def cond(st):
        j, carry, _ = st
        top = functools.reduce(jnp.maximum, [jnp.max(cp) for cp in carry])
        return jnp.logical_and(j >= 0, top > -SB_SKIP)

    def body(st):
        j, carry, acc = st
        carry, acc = tile(j, carry, acc, None)
        return j - 1, carry, acc

    _, _, acc = lax.while_loop(cond, body, (i - 1, carry, acc))
    for p in pairs:
        o_ref[0, :, p * LANES:(p + 1) * LANES] = jnp.where(head0, acc[p][:t], acc[p][t:]).astype(BF16)


def _sb_attn(p_main, kt):
    bsz, s, _ = p_main.shape
    n_grp = SB_HEADS // 2 // SB_GROUP
    gw = SB_GROUP * LANES
    nb = s // SB_T
    return pl.pallas_call(
        _sb_kernel,
        grid=(bsz, n_grp, nb),
        in_specs=[
            pl.BlockSpec((1, SB_T, gw), lambda b, h, i: (b, i, COL_CQ // gw + h)),
            pl.BlockSpec((1, SB_GROUP, nb, LANES, SB_T), lambda b, h, i: (b, h, 0, 0, 0)),
            pl.BlockSpec((1, s, gw), lambda b, h, i: (b, 0, COL_CV // gw + h)),
        ],
        out_specs=pl.BlockSpec((1, SB_T, gw), lambda b, h, i: (b, i, h)),
        out_shape=jax.ShapeDtypeStruct((bsz, s, BRANCH_W), BF16),
        compiler_params=_cparams(("parallel", "parallel", "arbitrary"), 32),
        name="sb_attn",
    )(p_main, kt, p_main)


MG_TM = 512


def _merge_kernel(x_ref, ga_ref, gb_ref, gc_ref, ya_ref, yb_ref, yc_ref, wb_ref, wo_ref, ng_ref, o_ref):
    m = None
    for n, (g_ref, y_ref) in enumerate(((ga_ref, ya_ref), (gb_ref, yb_ref), (gc_ref, yc_ref))):
        proj = _dot(y_ref[0], wb_ref[n])
        gate = _sigmoid(g_ref[0].astype(F32))
        m = gate * proj if m is None else m + gate * proj
    mixed = _dot(m.astype(BF16), wo_ref[...])
    o_ref[0] = x_ref[0] + _rms(mixed, ng_ref[...])


def _merge(x, p_main, ya, yb, yc, wb, wo, ng):
    bsz, s, d = x.shape
    w = BRANCH_W
    y_spec = pl.BlockSpec((1, MG_TM, w), lambda b, i: (b, i, 0))

    def gate_spec(n):
        return pl.BlockSpec((1, MG_TM, d), lambda b, i: (b, i, COL_GATE // d + n))

    return pl.pallas_call(
        _merge_kernel,
        grid=(bsz, s // MG_TM),
        in_specs=[
            pl.BlockSpec((1, MG_TM, d), lambda b, i: (b, i, 0)),
            gate_spec(0), gate_spec(1), gate_spec(2),
            y_spec, y_spec, y_spec,
            _const_spec((N_BRANCH, w, d)),
            _const_spec((d, d)),
            _const_spec((1, d)),
        ],
        out_specs=pl.BlockSpec((1, MG_TM, d), lambda b, i: (b, i, 0)),
        out_shape=jax.ShapeDtypeStruct((bsz, s, d), F32),
        compiler_params=_cparams(("parallel", "parallel"), 48),
        name="merge",
    )(x, p_main, p_main, p_main, ya, yb, yc, wb, wo, ng)


FF_TM = 512
FF_CW = 1024


def _ffn_kernel(x_ref, g1_ref, w1_ref, w2_ref, g2_ref, o_ref):
    x = x_ref[0]
    hb = _rms(x, g1_ref[...]).astype(BF16)
    dff = w1_ref.shape[1]
    f = None
    for c in range(0, dff, FF_CW):
        a = jnp.maximum(_dot(hb, w1_ref[:, c:c + FF_CW]), 0.0)
        part = _dot((a * a).astype(BF16), w2_ref[c:c + FF_CW, :])
        f = part if f is None else f + part
    o_ref[0] = x + _rms(f, g2_ref[...])


def _ffn(x, g1, w1, w2, g2):
    bsz, s, d = x.shape
    dff = w1.shape[1]
    return pl.pallas_call(
        _ffn_kernel,
        grid=(bsz, s // FF_TM),
        in_specs=[
            pl.BlockSpec((1, FF_TM, d), lambda b, i: (b, i, 0)),
            _const_spec((1, d)),
            _const_spec((d, dff)),
            _const_spec((dff, d)),
            _const_spec((1, d)),
        ],
        out_specs=pl.BlockSpec((1, FF_TM, d), lambda b, i: (b, i, 0)),
        out_shape=jax.ShapeDtypeStruct((bsz, s, d), F32),
        compiler_params=_cparams(("parallel", "parallel"), 56),
        name="ffn",
    )(x, g1, w1, w2, g2)


def _prep_in_proj(w, b):
    w_sz = BRANCH_W
    sizes = (w_sz,) * 6 + (DN_HEADS, DN_HEADS) + (w_sz,) * 3
    offs = [0]
    for sz in sizes:
        offs.append(offs[-1] + sz)

    def cols(t, n):
        return t[..., offs[n]:offs[n + 1]]

    def rest(t):
        return t[..., offs[-1]:]

    scale = SB_DIM ** -0.5
    order = lambda t: jnp.concatenate(
        [cols(t, 0), cols(t, 1), cols(t, 2), cols(t, 3), cols(t, 4), cols(t, 5),
         cols(t, 8) * scale, cols(t, 10), rest(t)], axis=-1)
    w_main = order(w).astype(BF16)
    b_main = order(b)[None, :]
    pad = LANES - 2 * DN_HEADS
    w_bd = jnp.pad(jnp.concatenate([cols(w, 6), cols(w, 7)], axis=-1), ((0, 0), (0, pad))).astype(BF16)
    b_bd = jnp.pad(jnp.concatenate([cols(b, 6), cols(b, 7)], axis=-1), (0, pad))[None, :]
    w_kt = cols(w, 9).T.astype(BF16)
    b_k = cols(b, 9)[:, None]
    return w_main, b_main, w_bd, b_bd, w_kt, b_k


def _lane_row(vals, start):
    return jnp.zeros((1, LANES), F32).at[0, start:start + vals.shape[0]].set(vals.astype(F32))


def kernel(x, norm_g, w_in, b_in, sgu_ln_g, sgu_ln_b, w_spatial, b_spatial, conv_w, a_log, dt_bias,
           dn_norm_g, w_branch, w_out, w_ff1, w_ff2):
    depth = norm_g.shape[0]
    x = x.astype(F32)
    for l in range(depth):
        w_main, b_main, w_bd, b_bd, w_kt, b_k = _prep_in_proj(w_in[l], b_in[l])
        p_main, bd, kt = _in_proj(x, norm_g[l, 0][None, :], w_main, b_main, w_bd, b_bd, w_kt, b_k)

        bs_full = jnp.repeat(b_spatial[l].T, BRANCH_W // GM_GROUPS, axis=1)
        y_a = _sgu(p_main, w_spatial[l], bs_full, sgu_ln_g[l][None, :], sgu_ln_b[l][None, :])

        y_b = _deltanet(p_main, bd, conv_w[l], _lane_row(a_log[l], DN_HEADS),
                        _lane_row(dt_bias[l], DN_HEADS), dn_norm_g[l][None, :])

        y_c = _sb_attn(p_main, kt)

        x = _merge(x, p_main, y_a, y_b, y_c, w_branch[l].astype(BF16), w_out[l].astype(BF16),
                   norm_g[l, 1][None, :])
        x = _ffn(x, norm_g[l, 2][None, :], w_ff1[l].astype(BF16), w_ff2[l].astype(BF16),
                 norm_g[l, 3][None, :])
    return x
```

```python
import functools

import jax
import jax.numpy as jnp
from jax import lax
from jax.experimental import pallas as pl
from jax.experimental.pallas import tpu as pltpu

F32 = jnp.float32
BF16 = jnp.bfloat16
NORM_EPS = 1e-6

LANES = 128
MIB = 1024 * 1024

BRANCH_W = 512
GM_CHUNK = 128
GM_GROUPS = 8
DN_HEADS = 4
DN_DIM = 128
DN_CHUNK = 64
CONV_W = 4
SB_HEADS = 8
SB_DIM = 64
N_BRANCH = 3

COL_AU, COL_AV, COL_BQ, COL_BK, COL_BV, COL_BZ, COL_CQ, COL_CV, COL_GATE = (
    0, 512, 1024, 1536, 2048, 2560, 3072, 3584, 4096)


SB_T = 256
LOG2E = 1.4426950408889634
SB_SKIP = 100.0 * LOG2E
SB_GROUP = 2


def _cparams(sem, vmem_mib):
    return pltpu.CompilerParams(dimension_semantics=sem, vmem_limit_bytes=vmem_mib * MIB)


def _const_spec(shape):
    nd = len(shape)
    return pl.BlockSpec(shape, lambda *_: (0,) * nd, pipeline_mode=pl.Buffered(1))


def _rms(x, g):
    return x * lax.rsqrt(jnp.mean(x * x, axis=-1, keepdims=True) + NORM_EPS) * g


def _sigmoid(x):
    return 1.0 / (1.0 + jnp.exp(-x))


def _softplus(x):
    return jnp.maximum(x, 0.0) + jnp.log(1.0 + jnp.exp(-jnp.abs(x)))


def _gelu(x):
    return 0.5 * x * (1.0 + lax.erf(x * (2.0 ** -0.5)))


def _dot(a, b):
    return jnp.dot(a, b, preferred_element_type=F32)


def _dot_nt(a, b):
    return lax.dot_general(a, b, (((1,), (1,)), ((), ())), preferred_element_type=F32)


IN_TM = 512
IN_CW = 1024


def _in_proj_kernel(x_ref, g_ref, w_ref, b_ref, wbd_ref, bbd_ref, wkt_ref, bk_ref,
                    p_ref, bd_ref, kt_ref):
    hb = _rms(x_ref[0], g_ref[...]).astype(BF16)
    p_cols = w_ref.shape[1]
    for c in range(0, p_cols, IN_CW):
        acc = _dot(hb, w_ref[:, c:c + IN_CW]) + b_ref[:, c:c + IN_CW]
        p_ref[0, :, c:c + IN_CW] = acc.astype(BF16)
    bd_ref[0] = _dot(hb, wbd_ref[...]) + bbd_ref[...]
    kt = _dot_nt(wkt_ref[...], hb) + bk_ref[...]
    for hp in range(SB_HEADS // 2):
        for jb in range(IN_TM // SB_T):
            kt_ref[0, hp, jb] = kt[hp * LANES:(hp + 1) * LANES,
                                   jb * SB_T:(jb + 1) * SB_T].astype(BF16)


def _in_proj(x, g, w_main, b_main, w_bd, b_bd, w_kt, b_k):
    bsz, s, d = x.shape
    pm = w_main.shape[1]
    n_hp = SB_HEADS // 2
    grid = (bsz, s // IN_TM)
    return pl.pallas_call(
        _in_proj_kernel,
        grid=grid,
        in_specs=[
            pl.BlockSpec((1, IN_TM, d), lambda b, i: (b, i, 0)),
            _const_spec((1, d)),
            _const_spec((d, pm)),
            _const_spec((1, pm)),
            _const_spec((d, LANES)),
            _const_spec((1, LANES)),
            _const_spec((BRANCH_W, d)),
            _const_spec((BRANCH_W, 1)),
        ],
        out_specs=[
            pl.BlockSpec((1, IN_TM, pm), lambda b, i: (b, i, 0)),
            pl.BlockSpec((1, IN_TM, LANES), lambda b, i: (b, i, 0)),
            pl.BlockSpec((1, n_hp, IN_TM // SB_T, LANES, SB_T), lambda b, i: (b, 0, i, 0, 0)),
        ],
        out_shape=[
            jax.ShapeDtypeStruct((bsz, s, pm), BF16),
            jax.ShapeDtypeStruct((bsz, s, LANES), F32),
            jax.ShapeDtypeStruct((bsz, n_hp, s // SB_T, LANES, SB_T), BF16),
        ],
        compiler_params=_cparams(("parallel", "parallel"), 56),
        name="in_proj",
    )(x, g, w_main, b_main, w_bd, b_bd, w_kt, b_k)


SGU_T = 256


def _sgu_kernel(u_ref, v_ref, ws_ref, bs_ref, lng_ref, lnb_ref, o_ref):
    u = _gelu(u_ref[0].astype(F32))
    v = _gelu(v_ref[0].astype(F32))
    vc = v - jnp.mean(v, axis=-1, keepdims=True)
    v = vc * lax.rsqrt(jnp.mean(vc * vc, axis=-1, keepdims=True) + NORM_EPS)
    v = (v * lng_ref[...] + lnb_ref[...]).astype(BF16)
    row = lax.broadcasted_iota(jnp.int32, (GM_CHUNK, GM_CHUNK), 0)
    col = lax.broadcasted_iota(jnp.int32, (GM_CHUNK, GM_CHUNK), 1)
    causal = col <= row
    first_half = col < (LANES // 2)
    ws = [jnp.where(causal, ws_ref[g], 0.0).astype(BF16) for g in range(GM_GROUPS)]
    for c in range(SGU_T // GM_CHUNK):
        r0 = c * GM_CHUNK
        for p in range(GM_GROUPS // 2):
            v2 = v[r0:r0 + GM_CHUNK, p * LANES:(p + 1) * LANES]
            mixed = jnp.where(first_half, _dot(ws[2 * p], v2), _dot(ws[2 * p + 1], v2))
            mixed = mixed + bs_ref[:, p * LANES:(p + 1) * LANES]
            y = u[r0:r0 + GM_CHUNK, p * LANES:(p + 1) * LANES] * mixed
            o_ref[0, r0:r0 + GM_CHUNK, p * LANES:(p + 1) * LANES] = y.astype(BF16)


def _sgu(p_main, ws, bs_full, lng, lnb):
    bsz, s, _ = p_main.shape
    w = BRANCH_W
    return pl.pallas_call(
        _sgu_kernel,
        grid=(bsz, s // SGU_T),
        in_specs=[
            pl.BlockSpec((1, SGU_T, w), lambda b, i: (b, i, COL_AU // w)),
            pl.BlockSpec((1, SGU_T, w), lambda b, i: (b, i, COL_AV // w)),
            _const_spec((GM_GROUPS, GM_CHUNK, GM_CHUNK)),
            _const_spec((GM_CHUNK, w)),
            _const_spec((1, w)),
            _const_spec((1, w)),
        ],
        out_specs=pl.BlockSpec((1, SGU_T, w), lambda b, i: (b, i, 0)),
        out_shape=jax.ShapeDtypeStruct((bsz, s, w), BF16),
        compiler_params=_cparams(("parallel", "parallel"), 32),
        name="sgu",
    )(p_main, p_main, ws, bs_full, lng, lnb)


DN_T = 256
DN_SUB = 32
HALO = 8


def _dn_kernel(q_ref, k_ref, v_ref, z_ref, qp_ref, kp_ref, vp_ref, bd_ref,
               cw_ref, alog_ref, dtb_ref, ng_ref, o_ref, state_ref):
    sb = pl.program_id(1)

    @pl.when(sb == 0)
    def _():
        state_ref[...] = jnp.zeros_like(state_ref)

    not_first = (sb > 0).astype(F32)

    halo_row = lax.broadcasted_iota(jnp.int32, (HALO, BRANCH_W), 0)

    def conv_silu(x_ref, xp_ref, part):
        x = x_ref[0].astype(F32)
        xp = xp_ref[0].astype(F32) * not_first
        taps = cw_ref[:, part * BRANCH_W:(part + 1) * BRANCH_W]
        y = taps[CONV_W - 1:CONV_W] * x
        for sft in range(1, CONV_W):
            xr = pltpu.roll(x, sft, 0)
            head = jnp.where(halo_row < sft, pltpu.roll(xp, sft, 0), xr[:HALO])
            xs = jnp.concatenate([head, xr[HALO:]], axis=0)
            y = y + taps[CONV_W - 1 - sft:CONV_W - sft] * xs
        return y * _sigmoid(y)

    q = conv_silu(q_ref, qp_ref, 0)
    k = conv_silu(k_ref, kp_ref, 1)
    v = conv_silu(v_ref, vp_ref, 2)
    z = z_ref[0].astype(F32)

    bd = bd_ref[0]
    lane = lax.broadcasted_iota(jnp.int32, bd.shape, 1)
    beta_all = _sigmoid(bd)
    g_all = -jnp.exp(alog_ref[...]) * _softplus(bd + dtb_ref[...])
    g_all = jnp.where((lane >= DN_HEADS) & (lane < 2 * DN_HEADS), g_all, 0.0)

    t = DN_T
    c = DN_CHUNK
    nc = t // c
    row = lax.broadcasted_iota(jnp.int32, (t, t), 0)
    col = lax.broadcasted_iota(jnp.int32, (t, t), 1)
    same_chunk = (row // c) == (col // c)
    tri = same_chunk & (col <= row)
    strict = same_chunk & (col < row)
    same_sub = (row // DN_SUB) == (col // DN_SUB)
    eye = (col == row).astype(F32)
    col_k =lax.broadcasted_iota(jnp.int32, (DN_DIM, t), 1) // c

    gc = jnp.dot(tri.astype(F32), g_all, precision=lax.Precision.HIGHEST,
                 preferred_element_type=F32)
    gct = gc.T

    heads = range(DN_HEADS)
    qn, kn, vn, kb = [], [], [], []
    gcol, bcol, egc, decay, glast, k_tt = [], [], [], [], [], []
    for h in heads:
        hs = slice(h * DN_DIM, (h + 1) * DN_DIM)
        qh, kh = q[:, hs], k[:, hs]
        qn.append(qh * lax.rsqrt(jnp.sum(qh * qh, axis=-1, keepdims=True) + NORM_EPS) * (DN_DIM ** -0.5))
        kn.append(kh * lax.rsqrt(jnp.sum(kh * kh, axis=-1, keepdims=True) + NORM_EPS))
        vn.append(v[:, hs])
        kb.append(kn[h].astype(BF16))
        gcol.append(gc[:, DN_HEADS + h:DN_HEADS + h + 1])
        grow = gct[DN_HEADS + h:DN_HEADS + h + 1, :]
        bcol.append(beta_all[:, h:h + 1])
        egc.append(jnp.exp(gcol[h]))
        decay.append(jnp.exp(jnp.where(tri, gcol[h] - grow, -1e30)))
        glast.append([gcol[h][ci * c + c - 1:ci * c + c, :] for ci in range(nc)])
        glast_rows = jnp.concatenate([jnp.broadcast_to(g, (c, 1)) for g in glast[h]], axis=0)
        k_tt.append((kn[h] * jnp.exp(glast_rows - gcol[h])).T)

    kk = [_dot_nt(kb[h], kb[h]) for h in heads]
    qk = [_dot_nt(qn[h].astype(BF16), kb[h]) for h in heads]
    qk = [jnp.where(tri, qk[h] * decay[h], 0.0) for h in heads]
    a = [jnp.where(strict, bcol[h] * kk[h] * decay[h], 0.0) for h in heads]
    a_d = [jnp.where(same_sub, a[h], 0.0) for h in heads]
    a_l = [jnp.where(same_sub, 0.0, a[h]).astype(BF16) for h in heads]
    x = [_dot(a_d[h].astype(BF16), a_d[h].astype(BF16)) for h in heads]
    t_d = [eye - a_d[h] for h in heads]
    for _ in range(DN_SUB.bit_length() - 3):
        both = [_dot(jnp.concatenate([x[h], t_d[h]], axis=0).astype(BF16), x[h].astype(BF16))
                for h in heads]
        x = [both[h][:t] for h in heads]
        t_d = [t_d[h] + both[h][t:] for h in heads]
    last = [_dot(t_d[h].astype(BF16), x[h].astype(BF16)) for h in heads]
    t_d = [(t_d[h] + last[h]).astype(BF16) for h in heads]
    rhs = [jnp.concatenate([vn[h] * bcol[h], kn[h] * (bcol[h] * egc[h])], axis=-1) for h in heads]
    sol = [_dot(t_d[h], rhs[h].astype(BF16)) for h in heads]
    for _ in range(c // DN_SUB - 1):
        below = [_dot(a_l[h], sol[h].astype(BF16)) for h in heads]
        sol = [_dot(t_d[h], (rhs[h] - below[h]).astype(BF16)) for h in heads]
    u_l = [sol[h][:, :DN_DIM] for h in heads]
    w_l = [sol[h][:, DN_DIM:] for h in heads]
    qd_l = [qn[h] * egc[h] for h in heads]

    state = [state_ref[h] for h in heads]
    for ci in range(nc):
        rs = slice(ci * c, (ci + 1) * c)
        r1 = [_dot(jnp.concatenate([w_l[h][rs], qd_l[h][rs]], axis=0).astype(BF16), state[h].astype(BF16))
              for h in heads]
        lhs2, v_pad = [], []
        for h in heads:
            pieces = []
            if ci > 0:
                pieces.append(jnp.zeros((ci * c, DN_DIM), F32))
            pieces.append(u_l[h][rs] - r1[h][:c])
            if ci < nc - 1:
                pieces.append(jnp.zeros(((nc - 1 - ci) * c, DN_DIM), F32))
            v_pad.append(jnp.concatenate(pieces, axis=0).astype(BF16))
            ktt_c = jnp.where(col_k == ci, k_tt[h], jnp.zeros_like(k_tt[h]))
            lhs2.append(jnp.concatenate([qk[h][rs, :], ktt_c], axis=0).astype(BF16))
        r2 = [_dot(lhs2[h], v_pad[h]) for h in heads]
        for h in heads:
            hs = slice(h * DN_DIM, (h + 1) * DN_DIM)
            state[h] = state[h] * jnp.exp(glast[h][ci]) + r2[h][c:]
            o = _rms(r1[h][c:] + r2[h][:c], ng_ref[...])
            zh = z[rs, hs]
            o_ref[0, rs, hs] = (o * (zh * _sigmoid(zh))).astype(BF16)
    for h in heads:
        state_ref[h] = state[h]


def _deltanet(p_main, bd, conv_w, alog_l, dtb_l, ng):
    bsz, s, _ = p_main.shape
    w = BRANCH_W
    hb = DN_T // HALO

    def cur(col):
        return pl.BlockSpec((1, DN_T, w), lambda b, i: (b, i, col // w))

    def prev(col):
        return pl.BlockSpec((1, HALO, w), lambda b, i: (b, jnp.maximum(i * hb - 1, 0), col // w))

    return pl.pallas_call(
        _dn_kernel,
        grid=(bsz, s // DN_T),
        in_specs=[
            cur(COL_BQ), cur(COL_BK), cur(COL_BV), cur(COL_BZ),
            prev(COL_BQ), prev(COL_BK), prev(COL_BV),
            pl.BlockSpec((1, DN_T, LANES), lambda b, i: (b, i, 0)),
            _const_spec((CONV_W, 3 * w)),
            _const_spec((1, LANES)),
            _const_spec((1, LANES)),
            _const_spec((1, DN_DIM)),
        ],
        out_specs=pl.BlockSpec((1, DN_T, w), lambda b, i: (b, i, 0)),
        out_shape=jax.ShapeDtypeStruct((bsz, s, w), BF16),
        scratch_shapes=[pltpu.VMEM((DN_HEADS, DN_DIM, DN_DIM), F32)],
        compiler_params=_cparams(("parallel", "arbitrary"), 32),
        name="deltanet",
    )(p_main, p_main, p_main, p_main, p_main, p_main, p_main, bd, conv_w, alog_l, dtb_l, ng)


def _sb_kernel(q_ref, kt_ref, v_ref, o_ref):
    i = pl.program_id(2)
    t = SB_T
    pairs = range(SB_GROUP)
    lane = lax.broadcasted_iota(jnp.int32, (t, LANES), 1)
    head0 = lane < SB_DIM
    qs = []
    for p in pairs:
        q2 = q_ref[0, :, p * LANES:(p + 1) * LANES]
        zq = jnp.zeros_like(q2)
        qs.append(jnp.concatenate([jnp.where(head0, q2, zq), jnp.where(head0, zq, q2)], axis=0))
    row = lax.broadcasted_iota(jnp.int32, (t, t), 0)
    col = lax.broadcasted_iota(jnp.int32, (t, t), 1)
    after = (row > col).astype(BF16)
    causal = jnp.concatenate([col < row, col < row], axis=0)

    def sweep(j_lo, n, mask_last, carry, acc):
        rows = pl.ds(pl.multiple_of(j_lo * t, t), n * t)
        z = [_dot(qs[p], jnp.concatenate([kt_ref[0, p, j_lo + m] for m in range(n)], axis=1))
             for p in pairs]
        log_keep, log_beta = [], []
        for p in pairs:
            lks, lbs = [], []
            for m in range(n):
                y = z[p][:, m * t:(m + 1) * t] * LOG2E
                ny = -y
                lk = jnp.minimum(ny, 0.0) - jnp.log2(1.0 + jnp.exp2(jnp.minimum(y, ny)))
                lbs.append(y + lk)
                if m == n - 1 and mask_last is not None:
                    lk = jnp.where(mask_last, lk, 0.0)
                lks.append(lk)
            log_keep.append(lks)
            log_beta.append(lbs)
        later = [_dot(jnp.concatenate(log_keep[p], axis=0).astype(BF16), after)
                 for p in pairs]
        w, new_carry = [], []
        for p in pairs:
            cp = carry[p]
            ws = [None] * n
            for m in reversed(range(n)):
                wp = jnp.exp2(log_beta[p][m] + later[p][m * 2 * t:(m + 1) * 2 * t] + cp)
                if m == n - 1 and mask_last is not None:
                    wp = jnp.where(mask_last, wp, 0.0)
                ws[m] = wp
                cp = cp + jnp.sum(log_keep[p][m], axis=1, keepdims=True)
            w.append(jnp.concatenate(ws, axis=1).astype(BF16))
            new_carry.append(cp)
        pv = [_dot(w[p], v_ref[0, rows, p * LANES:(p + 1) * LANES]) for p in pairs]
        return tuple(new_carry), tuple(acc[p] + pv[p] for p in pairs)

    def write(acc):
        for p in pairs:
            o_ref[0, :, p * LANES:(p + 1) * LANES] = jnp.where(head0, acc[p][:t], acc[p][t:]).astype(BF16)

    zero_carry = (jnp.zeros((2 * t, 1), F32),) * SB_GROUP
    zero_acc = (jnp.zeros((2 * t, LANES), F32),) * SB_GROUP

    @pl.when(i == 0)
    def _():
        write(sweep(0, 1, causal, zero_carry, zero_acc)[1])

    @pl.when(i > 0)
    def _():
        carry, acc = sweep(i - 1, 2, causal, zero_carry, zero_acc)

        def cond(st):
            j, carry, _ = st
            top = functools.reduce(jnp.maximum, [jnp.max(cp) for cp in carry])
            return jnp.logical_and(j >= 0, top > -SB_SKIP)

        def body(st):
            j, carry, acc = st
            carry, acc = sweep(j, 1, None, carry, acc)
            return j - 1, carry, acc

        write(lax.while_loop(cond, body, (i - 2, carry, acc))[2])


def _sb_attn(p_main, kt):
    bsz, s, _ = p_main.shape
    n_grp = SB_HEADS // 2 // SB_GROUP
    gw = SB_GROUP * LANES
    nb = s // SB_T
    return pl.pallas_call(
        _sb_kernel,
        grid=(bsz, n_grp, nb),
        in_specs=[
            pl.BlockSpec((1, SB_T, gw), lambda b, h, i: (b, i, COL_CQ // gw + h)),
            pl.BlockSpec((1, SB_GROUP, nb, LANES, SB_T), lambda b, h, i: (b, h, 0, 0, 0)),
            pl.BlockSpec((1, s, gw), lambda b, h, i: (b, 0, COL_CV // gw + h)),
        ],
        out_specs=pl.BlockSpec((1, SB_T, gw), lambda b, h, i: (b, i, h)),
        out_shape=jax.ShapeDtypeStruct((bsz, s, BRANCH_W), BF16),
        compiler_params=_cparams(("parallel", "parallel", "arbitrary"), 32),
        name="sb_attn",
    )(p_main, kt, p_main)


MG_TM = 512


def _merge_kernel(x_ref, ga_ref, gb_ref, gc_ref, ya_ref, yb_ref, yc_ref, wb_ref, wo_ref, ng_ref, o_ref):
    m = None
    for n, (g_ref, y_ref) in enumerate(((ga_ref, ya_ref), (gb_ref, yb_ref), (gc_ref, yc_ref))):
        proj = _dot(y_ref[0], wb_ref[n])
        gate = _sigmoid(g_ref[0].astype(F32))
        m = gate * proj if m is None else m + gate * proj
    mixed = _dot(m.astype(BF16), wo_ref[...])
    o_ref[0] = x_ref[0] + _rms(mixed, ng_ref[...])


def _merge(x, p_main, ya, yb, yc, wb, wo, ng):
    bsz, s, d = x.shape
    w = BRANCH_W
    y_spec = pl.BlockSpec((1, MG_TM, w), lambda b, i: (b, i, 0))

    def gate_spec(n):
        return pl.BlockSpec((1, MG_TM, d), lambda b, i: (b, i, COL_GATE // d + n))

    return pl.pallas_call(
        _merge_kernel,
        grid=(bsz, s // MG_TM),
        in_specs=[
            pl.BlockSpec((1, MG_TM, d), lambda b, i: (b, i, 0)),
            gate_spec(0), gate_spec(1), gate_spec(2),
            y_spec, y_spec, y_spec,
            _const_spec((N_BRANCH, w, d)),
            _const_spec((d, d)),
            _const_spec((1, d)),
        ],
        out_specs=pl.BlockSpec((1, MG_TM, d), lambda b, i: (b, i, 0)),
        out_shape=jax.ShapeDtypeStruct((bsz, s, d), F32),
        compiler_params=_cparams(("parallel", "parallel"), 48),
        name="merge",
    )(x, p_main, p_main, p_main, ya, yb, yc, wb, wo, ng)


FF_TM = 512
FF_CW = 1024


def _ffn_kernel(x_ref, g1_ref, w1_ref, w2_ref, g2_ref, o_ref):
    x = x_ref[0]
    hb = _rms(x, g1_ref[...]).astype(BF16)
    dff = w1_ref.shape[1]
    f = None
    for c in range(0, dff, FF_CW):
        a = jnp.maximum(_dot(hb, w1_ref[:, c:c + FF_CW]), 0.0)
        part = _dot((a * a).astype(BF16), w2_ref[c:c + FF_CW, :])
        f = part if f is None else f + part
    o_ref[0] = x + _rms(f, g2_ref[...])


def _ffn(x, g1, w1, w2, g2):
    bsz, s, d = x.shape
    dff = w1.shape[1]
    return pl.pallas_call(
        _ffn_kernel,
        grid=(bsz, s // FF_TM),
        in_specs=[
            pl.BlockSpec((1, FF_TM, d), lambda b, i: (b, i, 0)),
            _const_spec((1, d)),
            _const_spec((d, dff)),
            _const_spec((dff, d)),
            _const_spec((1, d)),
        ],
        out_specs=pl.BlockSpec((1, FF_TM, d), lambda b, i: (b, i, 0)),
        out_shape=jax.ShapeDtypeStruct((bsz, s, d), F32),
        compiler_params=_cparams(("parallel", "parallel"), 56),
        name="ffn",
    )(x, g1, w1, w2, g2)


def _prep_in_proj(w, b):
    w_sz = BRANCH_W
    sizes = (w_sz,) * 6 + (DN_HEADS, DN_HEADS) + (w_sz,) * 3
    offs = [0]
    for sz in sizes:
        offs.append(offs[-1] + sz)

    def cols(t, n):
        return t[..., offs[n]:offs[n + 1]]

    def rest(t):
        return t[..., offs[-1]:]

    scale = SB_DIM ** -0.5
    order = lambda t: jnp.concatenate(
        [cols(t, 0), cols(t, 1), cols(t, 2), cols(t, 3), cols(t, 4), cols(t, 5),
         cols(t, 8) * scale, cols(t, 10), rest(t)], axis=-1)
    w_main = order(w).astype(BF16)
    b_main = order(b)[None, :]
    pad = LANES - 2 * DN_HEADS
    w_bd = jnp.pad(jnp.concatenate([cols(w, 6), cols(w, 7)], axis=-1), ((0, 0), (0, pad))).astype(BF16)
    b_bd = jnp.pad(jnp.concatenate([cols(b, 6), cols(b, 7)], axis=-1), (0, pad))[None, :]
    w_kt = cols(w, 9).T.astype(BF16)
    b_k = cols(b, 9)[:, None]
    return w_main, b_main, w_bd, b_bd, w_kt, b_k


def _lane_row(vals, start):
    return jnp.zeros((1, LANES), F32).at[0, start:start + vals.shape[0]].set(vals.astype(F32))


def kernel(x, norm_g, w_in, b_in, sgu_ln_g, sgu_ln_b, w_spatial, b_spatial, conv_w, a_log, dt_bias,
           dn_norm_g, w_branch, w_out, w_ff1, w_ff2):
    depth = norm_g.shape[0]
    x = x.astype(F32)
    for l in range(depth):
        w_main, b_main, w_bd, b_bd, w_kt, b_k = _prep_in_proj(w_in[l], b_in[l])
        p_main, bd, kt = _in_proj(x, norm_g[l, 0][None, :], w_main, b_main, w_bd, b_bd, w_kt, b_k)

        bs_full = jnp.repeat(b_spatial[l].T, BRANCH_W // GM_GROUPS, axis=1)
        y_a = _sgu(p_main, w_spatial[l], bs_full, sgu_ln_g[l][None, :], sgu_ln_b[l][None, :])

        y_b = _deltanet(p_main, bd, conv_w[l], _lane_row(a_log[l], DN_HEADS),
                        _lane_row(dt_bias[l], DN_HEADS), dn_norm_g[l][None, :])

        y_c = _sb_attn(p_main, kt)

        x = _merge(x, p_main, y_a, y_b, y_c, w_branch[l].astype(BF16), w_out[l].astype(BF16),
                   norm_g[l, 1][None, :])
        x = _ffn(x, norm_g[l, 2][None, :], w_ff1[l].astype(BF16), w_ff2[l].astype(BF16),
                 norm_g[l, 3][None, :])
    return x
```

```python
import functools

import jax
import jax.numpy as jnp
from jax import lax
from jax.experimental import pallas as pl
from jax.experimental.pallas import tpu as pltpu

F32 = jnp.float32
BF16 = jnp.bfloat16
NORM_EPS = 1e-6

LANES = 128
MIB = 1024 * 1024

BRANCH_W = 512
GM_CHUNK = 128
GM_GROUPS = 8
DN_HEADS = 4
DN_DIM = 128
DN_CHUNK = 64
CONV_W = 4
SB_HEADS = 8
SB_DIM = 64
N_BRANCH = 3

COL_AU, COL_AV, COL_BQ, COL_BK, COL_BV, COL_BZ, COL_CQ, COL_CV, COL_GATE = (
    0, 512, 1024, 1536, 2048, 2560, 3072, 3584, 4096)


SB_T = 256
LOG2E = 1.4426950408889634
SB_SKIP = 100.0 * LOG2E
SB_GROUP = 2


def _cparams(sem, vmem_mib):
    return pltpu.CompilerParams(dimension_semantics=sem, vmem_limit_bytes=vmem_mib * MIB)


def _const_spec(shape):
    nd = len(shape)
    return pl.BlockSpec(shape, lambda *_: (0,) * nd, pipeline_mode=pl.Buffered(1))


def _rms(x, g):
    return x * lax.rsqrt(jnp.mean(x * x, axis=-1, keepdims=True) + NORM_EPS) * g


def _sigmoid(x):
    return 1.0 / (1.0 + jnp.exp(-x))


def _softplus(x):
    return jnp.maximum(x, 0.0) + jnp.log(1.0 + jnp.exp(-jnp.abs(x)))


def _gelu(x):
    return 0.5 * x * (1.0 + lax.erf(x * (2.0 ** -0.5)))


def _dot(a, b):
    return jnp.dot(a, b, preferred_element_type=F32)


def _dot_nt(a, b):
    return lax.dot_general(a, b, (((1,), (1,)), ((), ())), preferred_element_type=F32)


IN_TM = 512
IN_CW = 1024


SRC_BD = 6 * BRANCH_W
SRC_CQ = SRC_BD + 2 * DN_HEADS
SRC_CK = SRC_CQ + BRANCH_W
SRC_CV = SRC_CK + BRANCH_W
SRC_GATE = SRC_CV + BRANCH_W
TAIL_CK = COL_GATE - SRC_BD + N_BRANCH * 2 * BRANCH_W
TAIL_W = TAIL_CK + BRANCH_W
REALIGN_ROWS = 128


def _in_proj_kernel(x_ref, g_ref, w_ref, b_ref, p_ref, bd_ref, kt_ref, wt_ref, bt_ref):
    d = w_ref.shape[0]
    gate_w = N_BRANCH * 2 * BRANCH_W
    moves = ((SRC_CQ, 0, BRANCH_W, SB_DIM ** -0.5), (SRC_CV, BRANCH_W, BRANCH_W, 1.0),
             (SRC_GATE, COL_GATE - SRC_BD, gate_w, 1.0), (SRC_CK, TAIL_CK, BRANCH_W, 1.0))

    def shifted(ref, rows, src, width):
        lo = src // LANES * LANES
        hi = -(-(src + width) // LANES) * LANES
        return ref[rows, lo:hi].astype(F32)[:, src - lo:src - lo + width]

    @pl.when((pl.program_id(0) == 0) & (pl.program_id(1) == 0))
    def _():
        for src, dst, width, scale in moves:
            bt_ref[:, dst:dst + width] = shifted(b_ref, slice(None), src, width) * scale

            def body(r, carry, src=src, dst=dst, width=width, scale=scale):
                rows = pl.ds(pl.multiple_of(r * REALIGN_ROWS, REALIGN_ROWS), REALIGN_ROWS)
                wt_ref[rows, dst:dst + width] = (shifted(w_ref, rows, src, width) * scale).astype(BF16)
                return carry

            lax.fori_loop(0, d // REALIGN_ROWS, body, 0)

    hb = _rms(x_ref[0], g_ref[...]).astype(BF16)
    for c in range(0, SRC_BD, IN_CW):
        acc = _dot(hb, w_ref[:, c:c + IN_CW]) + b_ref[:, c:c + IN_CW]
        p_ref[0, :, c:c + IN_CW] = acc.astype(BF16)
    bd_ref[0] = _dot(hb, w_ref[:, SRC_BD:SRC_BD + LANES]) + b_ref[:, SRC_BD:SRC_BD + LANES]
    for c in range(0, TAIL_CK, IN_CW):
        acc = _dot(hb, wt_ref[:, c:c + IN_CW]) + bt_ref[:, c:c + IN_CW]
        p_ref[0, :, SRC_BD + c:SRC_BD + c + IN_CW] = acc.astype(BF16)
    kt = (_dot(hb, wt_ref[:, TAIL_CK:TAIL_W]) + bt_ref[:, TAIL_CK:TAIL_W]).T
    for hp in range(SB_HEADS // 2):
        for jb in range(IN_TM // SB_T):
            kt_ref[0, hp, jb] = kt[hp * LANES:(hp + 1) * LANES,
                                   jb * SB_T:(jb + 1) * SB_T].astype(BF16)


def _in_proj(x, g, w, b):
    bsz, s, d = x.shape
    p_in = w.shape[1]
    pm = SRC_BD + TAIL_CK
    n_hp = SB_HEADS // 2
    grid = (bsz, s // IN_TM)
    return pl.pallas_call(
        _in_proj_kernel,
        grid=grid,
        in_specs=[
            pl.BlockSpec((1, IN_TM, d), lambda b, i: (b, i, 0)),
            _const_spec((1, d)),
            _const_spec((d, p_in)),
            _const_spec((1, p_in)),
        ],
        out_specs=[
            pl.BlockSpec((1, IN_TM, pm), lambda b, i: (b, i, 0)),
            pl.BlockSpec((1, IN_TM, LANES), lambda b, i: (b, i, 0)),
            pl.BlockSpec((1, n_hp, IN_TM // SB_T, LANES, SB_T), lambda b, i: (b, 0, i, 0, 0)),
        ],
        out_shape=[
            jax.ShapeDtypeStruct((bsz, s, pm), BF16),
            jax.ShapeDtypeStruct((bsz, s, LANES), F32),
            jax.ShapeDtypeStruct((bsz, n_hp, s // SB_T, LANES, SB_T), BF16),
        ],
        scratch_shapes=[pltpu.VMEM((d, TAIL_W), BF16), pltpu.VMEM((1, TAIL_W), F32)],
        compiler_params=_cparams(("arbitrary", "arbitrary"), 60),
        name="in_proj",
    )(x, g, w, b)


SGU_T = 256


def _sgu_kernel(u_ref, v_ref, ws_ref, bs_ref, lng_ref, lnb_ref, o_ref):
    u = _gelu(u_ref[0].astype(F32))
    v = _gelu(v_ref[0].astype(F32))
    vc = v - jnp.mean(v, axis=-1, keepdims=True)
    v = vc * lax.rsqrt(jnp.mean(vc * vc, axis=-1, keepdims=True) + NORM_EPS)
    v = (v * lng_ref[...] + lnb_ref[...]).astype(BF16)
    row = lax.broadcasted_iota(jnp.int32, (GM_CHUNK, GM_CHUNK), 0)
    col = lax.broadcasted_iota(jnp.int32, (GM_CHUNK, GM_CHUNK), 1)
    causal = col <= row
    first_half = col < (LANES // 2)
    ws = [jnp.where(causal, ws_ref[g], 0.0).astype(BF16) for g in range(GM_GROUPS)]
    for c in range(SGU_T // GM_CHUNK):
        r0 = c * GM_CHUNK
        for p in range(GM_GROUPS // 2):
            v2 = v[r0:r0 + GM_CHUNK, p * LANES:(p + 1) * LANES]
            mixed = jnp.where(first_half, _dot(ws[2 * p], v2), _dot(ws[2 * p + 1], v2))
            mixed = mixed + bs_ref[:, p * LANES:(p + 1) * LANES]
            y = u[r0:r0 + GM_CHUNK, p * LANES:(p + 1) * LANES] * mixed
            o_ref[0, r0:r0 + GM_CHUNK, p * LANES:(p + 1) * LANES] = y.astype(BF16)


def _sgu(p_main, ws, bs_full, lng, lnb):
    bsz, s, _ = p_main.shape
    w = BRANCH_W
    return pl.pallas_call(
        _sgu_kernel,
        grid=(bsz, s // SGU_T),
        in_specs=[
            pl.BlockSpec((1, SGU_T, w), lambda b, i: (b, i, COL_AU // w)),
            pl.BlockSpec((1, SGU_T, w), lambda b, i: (b, i, COL_AV // w)),
            _const_spec((GM_GROUPS, GM_CHUNK, GM_CHUNK)),
            _const_spec((GM_CHUNK, w)),
            _const_spec((1, w)),
            _const_spec((1, w)),
        ],
        out_specs=pl.BlockSpec((1, SGU_T, w), lambda b, i: (b, i, 0)),
        out_shape=jax.ShapeDtypeStruct((bsz, s, w), BF16),
        compiler_params=_cparams(("parallel", "parallel"), 32),
        name="sgu",
    )(p_main, p_main, ws, bs_full, lng, lnb)


DN_T = 256
DN_SUB = 32
HALO = 8


def _dn_kernel(q_ref, k_ref, v_ref, z_ref, qp_ref, kp_ref, vp_ref, bd_ref,
               cw_ref, alog_ref, dtb_ref, ng_ref, o_ref, state_ref):
    sb = pl.program_id(1)

    @pl.when(sb == 0)
    def _():
        state_ref[...] = jnp.zeros_like(state_ref)

    not_first = (sb > 0).astype(F32)

    halo_row = lax.broadcasted_iota(jnp.int32, (HALO, BRANCH_W), 0)

    def conv_silu(x_ref, xp_ref, part):
        x = x_ref[0].astype(F32)
        xp = xp_ref[0].astype(F32) * not_first
        taps = cw_ref[:, part * BRANCH_W:(part + 1) * BRANCH_W]
        y = taps[CONV_W - 1:CONV_W] * x
        for sft in range(1, CONV_W):
            xr = pltpu.roll(x, sft, 0)
            head = jnp.where(halo_row < sft, pltpu.roll(xp, sft, 0), xr[:HALO])
            xs = jnp.concatenate([head, xr[HALO:]], axis=0)
            y = y + taps[CONV_W - 1 - sft:CONV_W - sft] * xs
        return y * _sigmoid(y)

    q = conv_silu(q_ref, qp_ref, 0)
    k = conv_silu(k_ref, kp_ref, 1)
    v = conv_silu(v_ref, vp_ref, 2)
    z = z_ref[0].astype(F32)

    bd = bd_ref[0]
    lane = lax.broadcasted_iota(jnp.int32, bd.shape, 1)
    beta_all = _sigmoid(bd)
    g_all = -jnp.exp(alog_ref[...]) * _softplus(bd + dtb_ref[...])
    g_all = jnp.where((lane >= DN_HEADS) & (lane < 2 * DN_HEADS), g_all, 0.0)

    t = DN_T
    c = DN_CHUNK
    nc = t // c
    row = lax.broadcasted_iota(jnp.int32, (t, t), 0)
    col = lax.broadcasted_iota(jnp.int32, (t, t), 1)
    same_chunk = (row // c) == (col // c)
    tri = same_chunk & (col <= row)
    strict = same_chunk & (col < row)
    same_sub = (row // DN_SUB) == (col // DN_SUB)
    eye = (col == row).astype(F32)
    col_k =lax.broadcasted_iota(jnp.int32, (DN_DIM, t), 1) // c

    gc = jnp.dot(tri.astype(F32), g_all, precision=lax.Precision.HIGHEST,
                 preferred_element_type=F32)
    gct = gc.T

    heads = range(DN_HEADS)
    qn, kn, vn, kb = [], [], [], []
    gcol, bcol, egc, decay, glast, k_tt = [], [], [], [], [], []
    for h in heads:
        hs = slice(h * DN_DIM, (h + 1) * DN_DIM)
        qh, kh = q[:, hs], k[:, hs]
        qn.append(qh * lax.rsqrt(jnp.sum(qh * qh, axis=-1, keepdims=True) + NORM_EPS) * (DN_DIM ** -0.5))
        kn.append(kh * lax.rsqrt(jnp.sum(kh * kh, axis=-1, keepdims=True) + NORM_EPS))
        vn.append(v[:, hs])
        kb.append(kn[h].astype(BF16))
        gcol.append(gc[:, DN_HEADS + h:DN_HEADS + h + 1])
        grow = gct[DN_HEADS + h:DN_HEADS + h + 1, :]
        bcol.append(beta_all[:, h:h + 1])
        egc.append(jnp.exp(gcol[h]))
        decay.append(jnp.exp(jnp.where(tri, gcol[h] - grow, -1e30)))
        glast.append([gcol[h][ci * c + c - 1:ci * c + c, :] for ci in range(nc)])
        glast_rows = jnp.concatenate([jnp.broadcast_to(g, (c, 1)) for g in glast[h]], axis=0)
        k_tt.append((kn[h] * jnp.exp(glast_rows - gcol[h])).T)

    kk = [_dot_nt(kb[h], kb[h]) for h in heads]
    qk = [_dot_nt(qn[h].astype(BF16), kb[h]) for h in heads]
    qk = [jnp.where(tri, qk[h] * decay[h], 0.0) for h in heads]
    a = [jnp.where(strict, bcol[h] * kk[h] * decay[h], 0.0) for h in heads]
    a_d = [jnp.where(same_sub, a[h], 0.0) for h in heads]
    a_l = [jnp.where(same_sub, 0.0, a[h]).astype(BF16) for h in heads]
    x = [_dot(a_d[h].astype(BF16), a_d[h].astype(BF16)) for h in heads]
    t_d = [eye - a_d[h] for h in heads]
    for _ in range(DN_SUB.bit_length() - 3):
        both = [_dot(jnp.concatenate([x[h], t_d[h]], axis=0).astype(BF16), x[h].astype(BF16))
                for h in heads]
        x = [both[h][:t] for h in heads]
        t_d = [t_d[h] + both[h][t:] for h in heads]
    last = [_dot(t_d[h].astype(BF16), x[h].astype(BF16)) for h in heads]
    t_d = [(t_d[h] + last[h]).astype(BF16) for h in heads]
    rhs = [jnp.concatenate([vn[h] * bcol[h], kn[h] * (bcol[h] * egc[h])], axis=-1) for h in heads]
    sol = [_dot(t_d[h], rhs[h].astype(BF16)) for h in heads]
    for _ in range(c // DN_SUB - 1):
        below = [_dot(a_l[h], sol[h].astype(BF16)) for h in heads]
        sol = [_dot(t_d[h], (rhs[h] - below[h]).astype(BF16)) for h in heads]
    u_l = [sol[h][:, :DN_DIM] for h in heads]
    w_l = [sol[h][:, DN_DIM:] for h in heads]
    qd_l = [qn[h] * egc[h] for h in heads]

    state = [state_ref[h] for h in heads]
    for ci in range(nc):
        rs = slice(ci * c, (ci + 1) * c)
        r1 = [_dot(jnp.concatenate([w_l[h][rs], qd_l[h][rs]], axis=0).astype(BF16), state[h].astype(BF16))
              for h in heads]
        lhs2, v_pad = [], []
        for h in heads:
            pieces = []
            if ci > 0:
                pieces.append(jnp.zeros((ci * c, DN_DIM), F32))
            pieces.append(u_l[h][rs] - r1[h][:c])
            if ci < nc - 1:
                pieces.append(jnp.zeros(((nc - 1 - ci) * c, DN_DIM), F32))
            v_pad.append(jnp.concatenate(pieces, axis=0).astype(BF16))
            ktt_c = jnp.where(col_k == ci, k_tt[h], jnp.zeros_like(k_tt[h]))
            lhs2.append(jnp.concatenate([qk[h][rs, :], ktt_c], axis=0).astype(BF16))
        r2 = [_dot(lhs2[h], v_pad[h]) for h in heads]
        for h in heads:
            hs = slice(h * DN_DIM, (h + 1) * DN_DIM)
            state[h] = state[h] * jnp.exp(glast[h][ci]) + r2[h][c:]
            o = _rms(r1[h][c:] + r2[h][:c], ng_ref[...])
            zh = z[rs, hs]
            o_ref[0, rs, hs] = (o * (zh * _sigmoid(zh))).astype(BF16)
    for h in heads:
        state_ref[h] = state[h]


def _deltanet(p_main, bd, conv_w, alog_l, dtb_l, ng):
    bsz, s, _ = p_main.shape
    w = BRANCH_W
    hb = DN_T // HALO

    def cur(col):
        return pl.BlockSpec((1, DN_T, w), lambda b, i: (b, i, col // w))

    def prev(col):
        return pl.BlockSpec((1, HALO, w), lambda b, i: (b, jnp.maximum(i * hb - 1, 0), col // w))

    return pl.pallas_call(
        _dn_kernel,
        grid=(bsz, s // DN_T),
        in_specs=[
            cur(COL_BQ), cur(COL_BK), cur(COL_BV), cur(COL_BZ),
            prev(COL_BQ), prev(COL_BK), prev(COL_BV),
            pl.BlockSpec((1, DN_T, LANES), lambda b, i: (b, i, 0)),
            _const_spec((CONV_W, 3 * w)),
            _const_spec((1, LANES)),
            _const_spec((1, LANES)),
            _const_spec((1, DN_DIM)),
        ],
        out_specs=pl.BlockSpec((1, DN_T, w), lambda b, i: (b, i, 0)),
        out_shape=jax.ShapeDtypeStruct((bsz, s, w), BF16),
        scratch_shapes=[pltpu.VMEM((DN_HEADS, DN_DIM, DN_DIM), F32)],
        compiler_params=_cparams(("parallel", "arbitrary"), 32),
        name="deltanet",
    )(p_main, p_main, p_main, p_main, p_main, p_main, p_main, bd, conv_w, alog_l, dtb_l, ng)


def _sb_kernel(q_ref, kt_ref, v_ref, o_ref):
    i = pl.program_id(2)
    t = SB_T
    pairs = range(SB_GROUP)
    lane = lax.broadcasted_iota(jnp.int32, (t, LANES), 1)
    head0 = lane < SB_DIM
    qs = []
    for p in pairs:
        q2 = q_ref[0, :, p * LANES:(p + 1) * LANES]
        zq = jnp.zeros_like(q2)
        qs.append(jnp.concatenate([jnp.where(head0, q2, zq), jnp.where(head0, zq, q2)], axis=0))
    row = lax.broadcasted_iota(jnp.int32, (t, t), 0)
    col = lax.broadcasted_iota(jnp.int32, (t, t), 1)
    after = (row > col).astype(BF16)
    causal = jnp.concatenate([col < row, col < row], axis=0)

    def sweep(j_lo, n, mask_last, carry, acc):
        rows = pl.ds(pl.multiple_of(j_lo * t, t), n * t)
        z = [_dot(qs[p], jnp.concatenate([kt_ref[0, p, j_lo + m] for m in range(n)], axis=1))
             for p in pairs]
        log_keep, log_beta = [], []
        for p in pairs:
            lks, lbs = [], []
            for m in range(n):
                y = z[p][:, m * t:(m + 1) * t] * LOG2E
                ny = -y
                lk = jnp.minimum(ny, 0.0) - jnp.log2(1.0 + jnp.exp2(jnp.minimum(y, ny)))
                lbs.append(y + lk)
                if m == n - 1 and mask_last is not None:
                    lk = jnp.where(mask_last, lk, 0.0)
                lks.append(lk)
            log_keep.append(lks)
            log_beta.append(lbs)
        later = [_dot(jnp.concatenate(log_keep[p], axis=0).astype(BF16), after)
                 for p in pairs]
        w, new_carry = [], []
        for p in pairs:
            cp = carry[p]
            ws = [None] * n
            for m in reversed(range(n)):
                wp = jnp.exp2(log_beta[p][m] + later[p][m * 2 * t:(m + 1) * 2 * t] + cp)
                if m == n - 1 and mask_last is not None:
                    wp = jnp.where(mask_last, wp, 0.0)
                ws[m] = wp
                cp = cp + jnp.sum(log_keep[p][m], axis=1, keepdims=True)
            w.append(jnp.concatenate(ws, axis=1).astype(BF16))
            new_carry.append(cp)
        pv = [_dot(w[p], v_ref[0, rows, p * LANES:(p + 1) * LANES]) for p in pairs]
        return tuple(new_carry), tuple(acc[p] + pv[p] for p in pairs)

    def write(acc):
        for p in pairs:
            o_ref[0, :, p * LANES:(p + 1) * LANES] = jnp.where(head0, acc[p][:t], acc[p][t:]).astype(BF16)

    zero_carry = (jnp.zeros((2 * t, 1), F32),) * SB_GROUP
    zero_acc = (jnp.zeros((2 * t, LANES), F32),) * SB_GROUP

    @pl.when(i == 0)
    def _():
        write(sweep(0, 1, causal, zero_carry, zero_acc)[1])

    @pl.when(i > 0)
    def _():
        carry, acc = sweep(i - 1, 2, causal, zero_carry, zero_acc)

        def cond(st):
            j, carry, _ = st
            top = functools.reduce(jnp.maximum, [jnp.max(cp) for cp in carry])
            return jnp.logical_and(j >= 0, top > -SB_SKIP)

        def body(st):
            j, carry, acc = st
            carry, acc = sweep(j, 1, None, carry, acc)
            return j - 1, carry, acc

        write(lax.while_loop(cond, body, (i - 2, carry, acc))[2])


def _sb_attn(p_main, kt):
    bsz, s, _ = p_main.shape
    n_grp = SB_HEADS // 2 // SB_GROUP
    gw = SB_GROUP * LANES
    nb = s // SB_T
    return pl.pallas_call(
        _sb_kernel,
        grid=(bsz, n_grp, nb),
        in_specs=[
            pl.BlockSpec((1, SB_T, gw), lambda b, h, i: (b, i, COL_CQ // gw + h)),
            pl.BlockSpec((1, SB_GROUP, nb, LANES, SB_T), lambda b, h, i: (b, h, 0, 0, 0)),
            pl.BlockSpec((1, s, gw), lambda b, h, i: (b, 0, COL_CV // gw + h)),
        ],
        out_specs=pl.BlockSpec((1, SB_T, gw), lambda b, h, i: (b, i, h)),
        out_shape=jax.ShapeDtypeStruct((bsz, s, BRANCH_W), BF16),
        compiler_params=_cparams(("parallel", "parallel", "arbitrary"), 32),
        name="sb_attn",
    )(p_main, kt, p_main)


MG_TM = 512


def _merge_kernel(x_ref, ga_ref, gb_ref, gc_ref, ya_ref, yb_ref, yc_ref, wb_ref, wo_ref, ng_ref, o_ref):
    m = None
    for n, (g_ref, y_ref) in enumerate(((ga_ref, ya_ref), (gb_ref, yb_ref), (gc_ref, yc_ref))):
        proj = _dot(y_ref[0], wb_ref[n])
        gate = _sigmoid(g_ref[0].astype(F32))
        m = gate * proj if m is None else m + gate * proj
    mixed = _dot(m.astype(BF16), wo_ref[...])
    o_ref[0] = x_ref[0] + _rms(mixed, ng_ref[...])


def _merge(x, p_main, ya, yb, yc, wb, wo, ng):
    bsz, s, d = x.shape
    w = BRANCH_W
    y_spec = pl.BlockSpec((1, MG_TM, w), lambda b, i: (b, i, 0))

    def gate_spec(n):
        return pl.BlockSpec((1, MG_TM, d), lambda b, i: (b, i, COL_GATE // d + n))

    return pl.pallas_call(
        _merge_kernel,
        grid=(bsz, s // MG_TM),
        in_specs=[
            pl.BlockSpec((1, MG_TM, d), lambda b, i: (b, i, 0)),
            gate_spec(0), gate_spec(1), gate_spec(2),
            y_spec, y_spec, y_spec,
            _const_spec((N_BRANCH, w, d)),
            _const_spec((d, d)),
            _const_spec((1, d)),
        ],
        out_specs=pl.BlockSpec((1, MG_TM, d), lambda b, i: (b, i, 0)),
        out_shape=jax.ShapeDtypeStruct((bsz, s, d), F32),
        compiler_params=_cparams(("parallel", "parallel"), 48),
        name="merge",
    )(x, p_main, p_main, p_main, ya, yb, yc, wb, wo, ng)


FF_TM = 512
FF_CW = 1024


def _ffn_kernel(x_ref, g1_ref, w1_ref, w2_ref, g2_ref, o_ref):
    x = x_ref[0]
    hb = _rms(x, g1_ref[...]).astype(BF16)
    dff = w1_ref.shape[1]
    f = None
    for c in range(0, dff, FF_CW):
        a = jnp.maximum(_dot(hb, w1_ref[:, c:c + FF_CW]), 0.0)
        part = _dot((a * a).astype(BF16), w2_ref[c:c + FF_CW, :])
        f = part if f is None else f + part
    o_ref[0] = x + _rms(f, g2_ref[...])


def _ffn(x, g1, w1, w2, g2):
    bsz, s, d = x.shape
    dff = w1.shape[1]
    return pl.pallas_call(
        _ffn_kernel,
        grid=(bsz, s // FF_TM),
        in_specs=[
            pl.BlockSpec((1, FF_TM, d), lambda b, i: (b, i, 0)),
            _const_spec((1, d)),
            _const_spec((d, dff)),
            _const_spec((dff, d)),
            _const_spec((1, d)),
        ],
        out_specs=pl.BlockSpec((1, FF_TM, d), lambda b, i: (b, i, 0)),
        out_shape=jax.ShapeDtypeStruct((bsz, s, d), F32),
        compiler_params=_cparams(("parallel", "parallel"), 56),
        name="ffn",
    )(x, g1, w1, w2, g2)


def _lane_row(vals, start):
    return jnp.zeros((1, LANES), F32).at[0, start:start + vals.shape[0]].set(vals.astype(F32))


def kernel(x, norm_g, w_in, b_in, sgu_ln_g, sgu_ln_b, w_spatial, b_spatial, conv_w, a_log, dt_bias,
           dn_norm_g, w_branch, w_out, w_ff1, w_ff2):
    depth = norm_g.shape[0]
    x = x.astype(F32)
    lane_pad = -w_in.shape[-1] % LANES
    w_in_b = jnp.pad(w_in.astype(BF16), ((0, 0), (0, 0), (0, lane_pad)))
    b_in_p = jnp.pad(b_in.astype(F32), ((0, 0), (0, lane_pad)))[:, None, :]
    for l in range(depth):
        p_main, bd, kt = _in_proj(x, norm_g[l, 0][None, :], w_in_b[l], b_in_p[l])

        bs_full = jnp.repeat(b_spatial[l].T, BRANCH_W // GM_GROUPS, axis=1)
        y_a = _sgu(p_main, w_spatial[l], bs_full, sgu_ln_g[l][None, :], sgu_ln_b[l][None, :])

        y_b = _deltanet(p_main, bd, conv_w[l], _lane_row(a_log[l], DN_HEADS),
                        _lane_row(dt_bias[l], DN_HEADS), dn_norm_g[l][None, :])

        y_c = _sb_attn(p_main, kt)

        x = _merge(x, p_main, y_a, y_b, y_c, w_branch[l].astype(BF16), w_out[l].astype(BF16),
                   norm_g[l, 1][None, :])
        x = _ffn(x, norm_g[l, 2][None, :], w_ff1[l].astype(BF16), w_ff2[l].astype(BF16),
                 norm_g[l, 3][None, :])
    return x
```

```python
import functools

import jax
import jax.numpy as jnp
from jax import lax
from jax.experimental import pallas as pl
from jax.experimental.pallas import tpu as pltpu

F32 = jnp.float32
BF16 = jnp.bfloat16
NORM_EPS = 1e-6

LANES = 128
MIB = 1024 * 1024

BRANCH_W = 512
GM_CHUNK = 128
GM_GROUPS = 8
DN_HEADS = 4
DN_DIM = 128
DN_CHUNK = 64
CONV_W = 4
SB_HEADS = 8
SB_DIM = 64
N_BRANCH = 3

COL_AU, COL_AV, COL_BQ, COL_BK, COL_BV, COL_BZ, COL_CQ, COL_CV, COL_GATE = (
    0, 512, 1024, 1536, 2048, 2560, 3072, 3584, 4096)


SB_T = 256
LOG2E = 1.4426950408889634
SB_SKIP = 100.0 * LOG2E
SB_GROUP = 4


def _cparams(sem, vmem_mib):
    return pltpu.CompilerParams(dimension_semantics=sem, vmem_limit_bytes=vmem_mib * MIB)


def _const_spec(shape):
    nd = len(shape)
    return pl.BlockSpec(shape, lambda *_: (0,) * nd, pipeline_mode=pl.Buffered(1))


def _layer_spec(shape, layer):
    nd = len(shape)
    return pl.BlockSpec((None,) + tuple(shape), lambda *_: (layer,) + (0,) * nd, pipeline_mode=pl.Buffered(1))


def _rms(x, g):
    return x * lax.rsqrt(jnp.mean(x * x, axis=-1, keepdims=True) + NORM_EPS) * g


def _sigmoid(x):
    return 1.0 / (1.0 + jnp.exp(-x))


def _softplus(x):
    return jnp.maximum(x, 0.0) + jnp.log(1.0 + jnp.exp(-jnp.abs(x)))


def _gelu(x):
    return 0.5 * x * (1.0 + lax.erf(x * (2.0 ** -0.5)))


def _dot(a, b):
    return jnp.dot(a, b, preferred_element_type=F32)


def _dot_nt(a, b):
    return lax.dot_general(a, b, (((1,), (1,)), ((), ())), preferred_element_type=F32)


IN_TM = 512
IN_CW = 1024


SRC_BD = 6 * BRANCH_W
SRC_CQ = SRC_BD + 2 * DN_HEADS
SRC_CK = SRC_CQ + BRANCH_W
SRC_CV = SRC_CK + BRANCH_W
SRC_GATE = SRC_CV + BRANCH_W
TAIL_CK = COL_GATE - SRC_BD + N_BRANCH * 2 * BRANCH_W
TAIL_W = TAIL_CK + BRANCH_W
REALIGN_ROWS = 128


def _in_proj_kernel(x_ref, g_ref, w_ref, b_ref, p_ref, bd_ref, kt_ref, wt_ref, bt_ref):
    d = w_ref.shape[0]
    gate_w = N_BRANCH * 2 * BRANCH_W
    moves = ((SRC_CQ, 0, BRANCH_W, SB_DIM ** -0.5), (SRC_CV, BRANCH_W, BRANCH_W, 1.0),
             (SRC_GATE, COL_GATE - SRC_BD, gate_w, 1.0), (SRC_CK, TAIL_CK, BRANCH_W, 1.0))

    def shifted(ref, rows, src, width):
        lo = src // LANES * LANES
        hi = -(-(src + width) // LANES) * LANES
        return ref[rows, lo:hi].astype(F32)[:, src - lo:src - lo + width]

    @pl.when((pl.program_id(0) == 0) & (pl.program_id(1) == 0))
    def _():
        for src, dst, width, scale in moves:
            bt_ref[:, dst:dst + width] = shifted(b_ref, slice(None), src, width) * scale

            def body(r, carry, src=src, dst=dst, width=width, scale=scale):
                rows = pl.ds(pl.multiple_of(r * REALIGN_ROWS, REALIGN_ROWS), REALIGN_ROWS)
                wt_ref[rows, dst:dst + width] = (shifted(w_ref, rows, src, width) * scale).astype(BF16)
                return carry

            lax.fori_loop(0, d // REALIGN_ROWS, body, 0)

    hb = _rms(x_ref[0], g_ref[...]).astype(BF16)
    for c in range(0, SRC_BD, IN_CW):
        acc = _dot(hb, w_ref[:, c:c + IN_CW]) + b_ref[:, c:c + IN_CW]
        p_ref[0, :, c:c + IN_CW] = acc.astype(BF16)
    bd_ref[0] = _dot(hb, w_ref[:, SRC_BD:SRC_BD + LANES]) + b_ref[:, SRC_BD:SRC_BD + LANES]
    for c in range(0, TAIL_CK, IN_CW):
        acc = _dot(hb, wt_ref[:, c:c + IN_CW]) + bt_ref[:, c:c + IN_CW]
        p_ref[0, :, SRC_BD + c:SRC_BD + c + IN_CW] = acc.astype(BF16)
    kt = (_dot(hb, wt_ref[:, TAIL_CK:TAIL_W]) + bt_ref[:, TAIL_CK:TAIL_W]).T
    for hp in range(SB_HEADS // 2):
        for jb in range(IN_TM // SB_T):
            kt_ref[0, hp, jb] = kt[hp * LANES:(hp + 1) * LANES,
                                   jb * SB_T:(jb + 1) * SB_T].astype(BF16)


def _in_proj(x, g, w, b, layer):
    bsz, s, d = x.shape
    p_in = w.shape[-1]
    pm = SRC_BD + TAIL_CK
    n_hp = SB_HEADS // 2
    grid = (bsz, s // IN_TM)
    return pl.pallas_call(
        _in_proj_kernel,
        grid=grid,
        in_specs=[
            pl.BlockSpec((1, IN_TM, d), lambda b, i: (b, i, 0)),
            _const_spec((1, d)),
            _layer_spec((d, p_in), layer),
            _layer_spec((1, p_in), layer),
        ],
        out_specs=[
            pl.BlockSpec((1, IN_TM, pm), lambda b, i: (b, i, 0)),
            pl.BlockSpec((1, IN_TM, LANES), lambda b, i: (b, i, 0)),
            pl.BlockSpec((1, n_hp, IN_TM // SB_T, LANES, SB_T), lambda b, i: (b, 0, i, 0, 0)),
        ],
        out_shape=[
            jax.ShapeDtypeStruct((bsz, s, pm), BF16),
            jax.ShapeDtypeStruct((bsz, s, LANES), F32),
            jax.ShapeDtypeStruct((bsz, n_hp, s // SB_T, LANES, SB_T), BF16),
        ],
        scratch_shapes=[pltpu.VMEM((d, TAIL_W), BF16), pltpu.VMEM((1, TAIL_W), F32)],
        compiler_params=_cparams(("arbitrary", "arbitrary"), 60),
        name="in_proj",
    )(x, g, w, b)


SGU_T = 256


def _sgu_kernel(u_ref, v_ref, ws_ref, bs_ref, lng_ref, lnb_ref, o_ref):
    u = _gelu(u_ref[0].astype(F32))
    v = _gelu(v_ref[0].astype(F32))
    vc = v - jnp.mean(v, axis=-1, keepdims=True)
    v = vc * lax.rsqrt(jnp.mean(vc * vc, axis=-1, keepdims=True) + NORM_EPS)
    v = (v * lng_ref[...] + lnb_ref[...]).astype(BF16)
    row = lax.broadcasted_iota(jnp.int32, (GM_CHUNK, GM_CHUNK), 0)
    col = lax.broadcasted_iota(jnp.int32, (GM_CHUNK, GM_CHUNK), 1)
    causal = col <= row
    first_half = col < (LANES // 2)
    ws = [jnp.where(causal, ws_ref[g], 0.0).astype(BF16) for g in range(GM_GROUPS)]
    for c in range(SGU_T // GM_CHUNK):
        r0 = c * GM_CHUNK
        for p in range(GM_GROUPS // 2):
            v2 = v[r0:r0 + GM_CHUNK, p * LANES:(p + 1) * LANES]
            mixed = jnp.where(first_half, _dot(ws[2 * p], v2), _dot(ws[2 * p + 1], v2))
            mixed = mixed + bs_ref[:, p * LANES:(p + 1) * LANES]
            y = u[r0:r0 + GM_CHUNK, p * LANES:(p + 1) * LANES] * mixed
            o_ref[0, r0:r0 + GM_CHUNK, p * LANES:(p + 1) * LANES] = y.astype(BF16)


def _sgu(p_main, ws, bs_full, lng, lnb):
    bsz, s, _ = p_main.shape
    w = BRANCH_W
    return pl.pallas_call(
        _sgu_kernel,
        grid=(bsz, s // SGU_T),
        in_specs=[
            pl.BlockSpec((1, SGU_T, w), lambda b, i: (b, i, COL_AU // w)),
            pl.BlockSpec((1, SGU_T, w), lambda b, i: (b, i, COL_AV // w)),
            _const_spec((GM_GROUPS, GM_CHUNK, GM_CHUNK)),
            _const_spec((GM_CHUNK, w)),
            _const_spec((1, w)),
            _const_spec((1, w)),
        ],
        out_specs=pl.BlockSpec((1, SGU_T, w), lambda b, i: (b, i, 0)),
        out_shape=jax.ShapeDtypeStruct((bsz, s, w), BF16),
        compiler_params=_cparams(("parallel", "parallel"), 32),
        name="sgu",
    )(p_main, p_main, ws, bs_full, lng, lnb)


DN_T = 256
DN_SUB = 32
HALO = 8


def _dn_kernel(q_ref, k_ref, v_ref, z_ref, qp_ref, kp_ref, vp_ref, bd_ref,
               cw_ref, alog_ref, dtb_ref, ng_ref, o_ref, state_ref):
    sb = pl.program_id(1)

    @pl.when(sb == 0)
    def _():
        state_ref[...] = jnp.zeros_like(state_ref)

    not_first = (sb > 0).astype(F32)

    halo_row = lax.broadcasted_iota(jnp.int32, (HALO, BRANCH_W), 0)

    def conv_silu(x_ref, xp_ref, part):
        x = x_ref[0].astype(F32)
        xp = xp_ref[0].astype(F32) * not_first
        taps = cw_ref[:, part * BRANCH_W:(part + 1) * BRANCH_W]
        y = taps[CONV_W - 1:CONV_W] * x
        for sft in range(1, CONV_W):
            xr = pltpu.roll(x, sft, 0)
            head = jnp.where(halo_row < sft, pltpu.roll(xp, sft, 0), xr[:HALO])
            xs = jnp.concatenate([head, xr[HALO:]], axis=0)
            y = y + taps[CONV_W - 1 - sft:CONV_W - sft] * xs
        return y * _sigmoid(y)

    q = conv_silu(q_ref, qp_ref, 0)
    k = conv_silu(k_ref, kp_ref, 1)
    v = conv_silu(v_ref, vp_ref, 2)
    z = z_ref[0].astype(F32)

    bd = bd_ref[0]
    lane = lax.broadcasted_iota(jnp.int32, bd.shape, 1)
    beta_all = _sigmoid(bd)
    g_all = -jnp.exp(alog_ref[...]) * _softplus(bd + dtb_ref[...])
    g_all = jnp.where((lane >= DN_HEADS) & (lane < 2 * DN_HEADS), g_all, 0.0)

    t = DN_T
    c = DN_CHUNK
    nc = t // c
    row = lax.broadcasted_iota(jnp.int32, (t, t), 0)
    col = lax.broadcasted_iota(jnp.int32, (t, t), 1)
    same_chunk = (row // c) == (col // c)
    tri = same_chunk & (col <= row)
    strict = same_chunk & (col < row)
    same_sub = (row // DN_SUB) == (col // DN_SUB)
    eye = (col == row).astype(F32)
    col_k =lax.broadcasted_iota(jnp.int32, (DN_DIM, t), 1) // c

    gc = jnp.dot(tri.astype(F32), g_all, precision=lax.Precision.HIGHEST,
                 preferred_element_type=F32)
    gct = gc.T

    heads = range(DN_HEADS)
    qn, kn, vn, kb = [], [], [], []
    gcol, bcol, egc, decay, glast, k_tt = [], [], [], [], [], []
    for h in heads:
        hs = slice(h * DN_DIM, (h + 1) * DN_DIM)
        qh, kh = q[:, hs], k[:, hs]
        qn.append(qh * lax.rsqrt(jnp.sum(qh * qh, axis=-1, keepdims=True) + NORM_EPS) * (DN_DIM ** -0.5))
        kn.append(kh * lax.rsqrt(jnp.sum(kh * kh, axis=-1, keepdims=True) + NORM_EPS))
        vn.append(v[:, hs])
        kb.append(kn[h].astype(BF16))
        gcol.append(gc[:, DN_HEADS + h:DN_HEADS + h + 1])
        grow = gct[DN_HEADS + h:DN_HEADS + h + 1, :]
        bcol.append(beta_all[:, h:h + 1])
        egc.append(jnp.exp(gcol[h]))
        decay.append(jnp.exp(jnp.where(tri, gcol[h] - grow, -1e30)))
        glast.append([gcol[h][ci * c + c - 1:ci * c + c, :] for ci in range(nc)])
        glast_rows = jnp.concatenate([jnp.broadcast_to(g, (c, 1)) for g in glast[h]], axis=0)
        k_tt.append((kn[h] * jnp.exp(glast_rows - gcol[h])).T)

    kk = [_dot_nt(kb[h], kb[h]) for h in heads]
    qk = [_dot_nt(qn[h].astype(BF16), kb[h]) for h in heads]
    qk = [jnp.where(tri, qk[h] * decay[h], 0.0) for h in heads]
    a = [jnp.where(strict, bcol[h] * kk[h] * decay[h], 0.0) for h in heads]
    a_d = [jnp.where(same_sub, a[h], 0.0) for h in heads]
    a_l = [jnp.where(same_sub, 0.0, a[h]).astype(BF16) for h in heads]
    x = [_dot(a_d[h].astype(BF16), a_d[h].astype(BF16)) for h in heads]
    t_d = [eye - a_d[h] for h in heads]
    for _ in range(DN_SUB.bit_length() - 3):
        both = [_dot(jnp.concatenate([x[h], t_d[h]], axis=0).astype(BF16), x[h].astype(BF16))
                for h in heads]
        x = [both[h][:t] for h in heads]
        t_d = [t_d[h] + both[h][t:] for h in heads]
    last = [_dot(t_d[h].astype(BF16), x[h].astype(BF16)) for h in heads]
    t_d = [(t_d[h] + last[h]).astype(BF16) for h in heads]
    rhs = [jnp.concatenate([vn[h] * bcol[h], kn[h] * (bcol[h] * egc[h])], axis=-1) for h in heads]
    sol = [_dot(t_d[h], rhs[h].astype(BF16)) for h in heads]
    for _ in range(c // DN_SUB - 1):
        below = [_dot(a_l[h], sol[h].astype(BF16)) for h in heads]
        sol = [_dot(t_d[h], (rhs[h] - below[h]).astype(BF16)) for h in heads]
    u_l = [sol[h][:, :DN_DIM] for h in heads]
    w_l = [sol[h][:, DN_DIM:] for h in heads]
    qd_l = [qn[h] * egc[h] for h in heads]

    state = [state_ref[h] for h in heads]
    for ci in range(nc):
        rs = slice(ci * c, (ci + 1) * c)
        r1 = [_dot(jnp.concatenate([w_l[h][rs], qd_l[h][rs]], axis=0).astype(BF16), state[h].astype(BF16))
              for h in heads]
        lhs2, v_pad = [], []
        for h in heads:
            pieces = []
            if ci > 0:
                pieces.append(jnp.zeros((ci * c, DN_DIM), F32))
            pieces.append(u_l[h][rs] - r1[h][:c])
            if ci < nc - 1:
                pieces.append(jnp.zeros(((nc - 1 - ci) * c, DN_DIM), F32))
            v_pad.append(jnp.concatenate(pieces, axis=0).astype(BF16))
            ktt_c = jnp.where(col_k == ci, k_tt[h], jnp.zeros_like(k_tt[h]))
            lhs2.append(jnp.concatenate([qk[h][rs, :], ktt_c], axis=0).astype(BF16))
        r2 = [_dot(lhs2[h], v_pad[h]) for h in heads]
        for h in heads:
            hs = slice(h * DN_DIM, (h + 1) * DN_DIM)
            state[h] = state[h] * jnp.exp(glast[h][ci]) + r2[h][c:]
            o = _rms(r1[h][c:] + r2[h][:c], ng_ref[...])
            zh = z[rs, hs]
            o_ref[0, rs, hs] = (o * (zh * _sigmoid(zh))).astype(BF16)
    for h in heads:
        state_ref[h] = state[h]


def _deltanet(p_main, bd, conv_w, alog_l, dtb_l, ng):
    bsz, s, _ = p_main.shape
    w = BRANCH_W
    hb = DN_T // HALO

    def cur(col):
        return pl.BlockSpec((1, DN_T, w), lambda b, i: (b, i, col // w))

    def prev(col):
        return pl.BlockSpec((1, HALO, w), lambda b, i: (b, jnp.maximum(i * hb - 1, 0), col // w))

    return pl.pallas_call(
        _dn_kernel,
        grid=(bsz, s // DN_T),
        in_specs=[
            cur(COL_BQ), cur(COL_BK), cur(COL_BV), cur(COL_BZ),
            prev(COL_BQ), prev(COL_BK), prev(COL_BV),
            pl.BlockSpec((1, DN_T, LANES), lambda b, i: (b, i, 0)),
            _const_spec((CONV_W, 3 * w)),
            _const_spec((1, LANES)),
            _const_spec((1, LANES)),
            _const_spec((1, DN_DIM)),
        ],
        out_specs=pl.BlockSpec((1, DN_T, w), lambda b, i: (b, i, 0)),
        out_shape=jax.ShapeDtypeStruct((bsz, s, w), BF16),
        scratch_shapes=[pltpu.VMEM((DN_HEADS, DN_DIM, DN_DIM), F32)],
        compiler_params=_cparams(("parallel", "arbitrary"), 32),
        name="deltanet",
    )(p_main, p_main, p_main, p_main, p_main, p_main, p_main, bd, conv_w, alog_l, dtb_l, ng)


def _sb_kernel(q_ref, kt_ref, v_ref, o_ref):
    i = pl.program_id(2)
    t = SB_T
    pairs = range(SB_GROUP)
    lane = lax.broadcasted_iota(jnp.int32, (t, LANES), 1)
    head0 = lane < SB_DIM
    qs = []
    for p in pairs:
        q2 = q_ref[0, :, p * LANES:(p + 1) * LANES]
        zq = jnp.zeros_like(q2)
        qs.append(jnp.concatenate([jnp.where(head0, q2, zq), jnp.where(head0, zq, q2)], axis=0))
    row = lax.broadcasted_iota(jnp.int32, (t, t), 0)
    col = lax.broadcasted_iota(jnp.int32, (t, t), 1)
    after = (row > col).astype(BF16)
    causal = jnp.concatenate([col < row, col < row], axis=0)

    def sweep(j_lo, n, mask_last, carry, acc):
        rows = pl.ds(pl.multiple_of(j_lo * t, t), n * t)
        z = [_dot(qs[p], jnp.concatenate([kt_ref[0, p, j_lo + m] for m in range(n)], axis=1))
             for p in pairs]
        log_keep, log_beta = [], []
        for p in pairs:
            lks, lbs = [], []
            for m in range(n):
                y = z[p][:, m * t:(m + 1) * t] * LOG2E
                ny = -y
                lk = jnp.minimum(ny, 0.0) - jnp.log2(1.0 + jnp.exp2(jnp.minimum(y, ny)))
                lbs.append(y + lk)
                if m == n - 1 and mask_last is not None:
                    lk = jnp.where(mask_last, lk, 0.0)
                lks.append(lk)
            log_keep.append(lks)
            log_beta.append(lbs)
        later = [_dot(jnp.concatenate(log_keep[p], axis=0).astype(BF16), after)
                 for p in pairs]
        w, new_carry = [], []
        for p in pairs:
            cp = carry[p]
            ws = [None] * n
            for m in reversed(range(n)):
                wp = jnp.exp2(log_beta[p][m] + later[p][m * 2 * t:(m + 1) * 2 * t] + cp)
                if m == n - 1 and mask_last is not None:
                    wp = jnp.where(mask_last, wp, 0.0)
                ws[m] = wp
                cp = cp + jnp.sum(log_keep[p][m], axis=1, keepdims=True)
            w.append(jnp.concatenate(ws, axis=1).astype(BF16))
            new_carry.append(cp)
        pv = [_dot(w[p], v_ref[0, rows, p * LANES:(p + 1) * LANES]) for p in pairs]
        return tuple(new_carry), tuple(acc[p] + pv[p] for p in pairs)

    def write(acc):
        for p in pairs:
            o_ref[0, :, p * LANES:(p + 1) * LANES] = jnp.where(head0, acc[p][:t], acc[p][t:]).astype(BF16)

    zero_carry = (jnp.zeros((2 * t, 1), F32),) * SB_GROUP
    zero_acc = (jnp.zeros((2 * t, LANES), F32),) * SB_GROUP

    @pl.when(i == 0)
    def _():
        write(sweep(0, 1, causal, zero_carry, zero_acc)[1])

    @pl.when(i > 0)
    def _():
        carry, acc = sweep(i - 1, 2, causal, zero_carry, zero_acc)

        def cond(st):
            j, carry, _ = st
            top = functools.reduce(jnp.maximum, [jnp.max(cp) for cp in carry])
            return jnp.logical_and(j >= 0, top > -SB_SKIP)

        def body(st):
            j, carry, acc = st
            carry, acc = sweep(j, 1, None, carry, acc)
            return j - 1, carry, acc

        write(lax.while_loop(cond, body, (i - 2, carry, acc))[2])


def _sb_attn(p_main, kt):
    bsz, s, _ = p_main.shape
    n_grp = SB_HEADS // 2 // SB_GROUP
    gw = SB_GROUP * LANES
    nb = s // SB_T
    return pl.pallas_call(
        _sb_kernel,
        grid=(bsz, n_grp, nb),
        in_specs=[
            pl.BlockSpec((1, SB_T, gw), lambda b, h, i: (b, i, COL_CQ // gw + h)),
            pl.BlockSpec((1, SB_GROUP, nb, LANES, SB_T), lambda b, h, i: (b, h, 0, 0, 0),
                         pipeline_mode=pl.Buffered(1)),
            pl.BlockSpec((1, s, gw), lambda b, h, i: (b, 0, COL_CV // gw + h), pipeline_mode=pl.Buffered(1)),
        ],
        out_specs=pl.BlockSpec((1, SB_T, gw), lambda b, h, i: (b, i, h)),
        out_shape=jax.ShapeDtypeStruct((bsz, s, BRANCH_W), BF16),
        compiler_params=_cparams(("parallel", "parallel", "arbitrary"), 48),
        name="sb_attn",
    )(p_main, kt, p_main)


MG_TM = 512


def _merge_kernel(x_ref, ga_ref, gb_ref, gc_ref, ya_ref, yb_ref, yc_ref, wb_ref, wo_ref, ng_ref, o_ref):
    m = None
    for n, (g_ref, y_ref) in enumerate(((ga_ref, ya_ref), (gb_ref, yb_ref), (gc_ref, yc_ref))):
        proj = _dot(y_ref[0], wb_ref[n])
        gate = _sigmoid(g_ref[0].astype(F32))
        m = gate * proj if m is None else m + gate * proj
    mixed = _dot(m.astype(BF16), wo_ref[...])
    o_ref[0] = x_ref[0] + _rms(mixed, ng_ref[...])


FF_CW = 1024


def _ffn_kernel(x_ref, g1_ref, w1_ref, w2_ref, g2_ref, o_ref):
    x = x_ref[0]
    hb = _rms(x, g1_ref[...]).astype(BF16)
    dff = w1_ref.shape[1]
    f = None
    for c in range(0, dff, FF_CW):
        a = jnp.maximum(_dot(hb, w1_ref[:, c:c + FF_CW]), 0.0)
        part = _dot((a * a).astype(BF16), w2_ref[c:c + FF_CW, :])
        f = part if f is None else f + part
    o_ref[0] = x + _rms(f, g2_ref[...])


def _merge_ffn_kernel(x_ref, ga_ref, gb_ref, gc_ref, ya_ref, yb_ref, yc_ref, wb_ref, wo_ref, ng_ref,
                      g1_ref, w1_ref, w2_ref, g2_ref, o_ref, xm_ref):
    _merge_kernel(x_ref, ga_ref, gb_ref, gc_ref, ya_ref, yb_ref, yc_ref, wb_ref, wo_ref, ng_ref, xm_ref)
    _ffn_kernel(xm_ref, g1_ref, w1_ref, w2_ref, g2_ref, o_ref)


def _merge_ffn(x, p_main, ya, yb, yc, wb, wo, ng, g1, w1, w2, g2, layer):
    bsz, s, d = x.shape
    w = BRANCH_W
    dff = w1.shape[-1]
    y_spec = pl.BlockSpec((1, MG_TM, w), lambda b, i: (b, i, 0))

    def gate_spec(n):
        return pl.BlockSpec((1, MG_TM, d), lambda b, i: (b, i, COL_GATE // d + n))

    return pl.pallas_call(
        _merge_ffn_kernel,
        grid=(bsz, s // MG_TM),
        in_specs=[
            pl.BlockSpec((1, MG_TM, d), lambda b, i: (b, i, 0)),
            gate_spec(0), gate_spec(1), gate_spec(2),
            y_spec, y_spec, y_spec,
            _layer_spec((N_BRANCH, w, d), layer),
            _layer_spec((d, d), layer),
            _const_spec((1, d)),
            _const_spec((1, d)),
            _layer_spec((d, dff), layer),
            _layer_spec((dff, d), layer),
            _const_spec((1, d)),
        ],
        out_specs=pl.BlockSpec((1, MG_TM, d), lambda b, i: (b, i, 0)),
        out_shape=jax.ShapeDtypeStruct((bsz, s, d), F32),
        scratch_shapes=[pltpu.VMEM((1, MG_TM, d), F32)],
        compiler_params=_cparams(("parallel", "parallel"), 60),
        name="merge_ffn",
    )(x, p_main, p_main, p_main, ya, yb, yc, wb, wo, ng, g1, w1, w2, g2)


def _lane_row(vals, start):
    return jnp.zeros((1, LANES), F32).at[0, start:start + vals.shape[0]].set(vals.astype(F32))


def kernel(x, norm_g, w_in, b_in, sgu_ln_g, sgu_ln_b, w_spatial, b_spatial, conv_w, a_log, dt_bias,
           dn_norm_g, w_branch, w_out, w_ff1, w_ff2):
    depth = norm_g.shape[0]
    x = x.astype(F32)
    lane_pad = -w_in.shape[-1] % LANES
    w_in_b = jnp.pad(w_in, ((0, 0), (0, 0), (0, lane_pad))).astype(BF16)
    b_in_p = jnp.pad(b_in.astype(F32), ((0, 0), (0, lane_pad)))[:, None, :]
    w_branch_b, w_out_b = w_branch.astype(BF16), w_out.astype(BF16)
    w_ff1_b, w_ff2_b = w_ff1.astype(BF16), w_ff2.astype(BF16)
    for l in range(depth):
        p_main, bd, kt = _in_proj(x, norm_g[l, 0][None, :], w_in_b, b_in_p, l)

        bs_full = jnp.repeat(b_spatial[l].T, BRANCH_W // GM_GROUPS, axis=1)
        y_a = _sgu(p_main, w_spatial[l], bs_full, sgu_ln_g[l][None, :], sgu_ln_b[l][None, :])

        y_b = _deltanet(p_main, bd, conv_w[l], _lane_row(a_log[l], DN_HEADS),
                        _lane_row(dt_bias[l], DN_HEADS), dn_norm_g[l][None, :])

        y_c = _sb_attn(p_main, kt)

        x = _merge_ffn(x, p_main, y_a, y_b, y_c, w_branch_b, w_out_b, norm_g[l, 1][None, :],
                       norm_g[l, 2][None, :], w_ff1_b, w_ff2_b, norm_g[l, 3][None, :], l)
    return x
```

```python
import functools

import jax
import jax.numpy as jnp
from jax import lax
from jax.experimental import pallas as pl
from jax.experimental.pallas import tpu as pltpu

F32 = jnp.float32
BF16 = jnp.bfloat16
NORM_EPS = 1e-6

LANES = 128
MIB = 1024 * 1024

BRANCH_W = 512
GM_CHUNK = 128
GM_GROUPS = 8
DN_HEADS = 4
DN_DIM = 128
DN_CHUNK = 64
CONV_W = 4
SB_HEADS = 8
SB_DIM = 64
N_BRANCH = 3

COL_AU, COL_AV, COL_BQ, COL_BK, COL_BV, COL_BZ, COL_CQ, COL_CV, COL_GATE = (
    0, 512, 1024, 1536, 2048, 2560, 3072, 3584, 4096)


SB_T = 256
LOG2E = 1.4426950408889634
SB_SKIP = 100.0 * LOG2E
SB_GROUP = 4


def _cparams(sem, vmem_mib):
    return pltpu.CompilerParams(dimension_semantics=sem, vmem_limit_bytes=vmem_mib * MIB)


def _const_spec(shape):
    nd = len(shape)
    return pl.BlockSpec(shape, lambda *_: (0,) * nd, pipeline_mode=pl.Buffered(1))


def _layer_spec(shape, layer):
    nd = len(shape)
    return pl.BlockSpec((None,) + tuple(shape), lambda *_: (layer,) + (0,) * nd, pipeline_mode=pl.Buffered(1))


def _rms(x, g):
    return x * lax.rsqrt(jnp.mean(x * x, axis=-1, keepdims=True) + NORM_EPS) * g


def _sigmoid(x):
    return 1.0 / (1.0 + jnp.exp(-x))


def _softplus(x):
    return jnp.maximum(x, 0.0) + jnp.log(1.0 + jnp.exp(-jnp.abs(x)))


def _gelu(x):
    return 0.5 * x * (1.0 + lax.erf(x * (2.0 ** -0.5)))


def _dot(a, b):
    return jnp.dot(a, b, preferred_element_type=F32)


def _dot_nt(a, b):
    return lax.dot_general(a, b, (((1,), (1,)), ((), ())), preferred_element_type=F32)


IN_TM = 512
IN_CW = 1024


SRC_BD = 6 * BRANCH_W
SRC_CQ = SRC_BD + 2 * DN_HEADS
SRC_CK = SRC_CQ + BRANCH_W
SRC_CV = SRC_CK + BRANCH_W
SRC_GATE = SRC_CV + BRANCH_W
TAIL_CK = COL_GATE - SRC_BD + N_BRANCH * 2 * BRANCH_W
TAIL_W = TAIL_CK + BRANCH_W
REALIGN_ROWS = 128


def _in_proj_kernel(x_ref, g_ref, w_ref, b_ref, p_ref, bd_ref, kt_ref, wt_ref, bt_ref):
    d = w_ref.shape[0]
    gate_w = N_BRANCH * 2 * BRANCH_W
    moves = ((SRC_CQ, 0, BRANCH_W, SB_DIM ** -0.5), (SRC_CV, BRANCH_W, BRANCH_W, 1.0),
             (SRC_GATE, COL_GATE - SRC_BD, gate_w, 1.0), (SRC_CK, TAIL_CK, BRANCH_W, 1.0))

    def shifted(ref, rows, src, width):
        lo = src // LANES * LANES
        hi = -(-(src + width) // LANES) * LANES
        return ref[rows, lo:hi].astype(F32)[:, src - lo:src - lo + width]

    @pl.when((pl.program_id(0) == 0) & (pl.program_id(1) == 0))
    def _():
        for src, dst, width, scale in moves:
            bt_ref[:, dst:dst + width] = shifted(b_ref, slice(None), src, width) * scale

            def body(r, carry, src=src, dst=dst, width=width, scale=scale):
                rows = pl.ds(pl.multiple_of(r * REALIGN_ROWS, REALIGN_ROWS), REALIGN_ROWS)
                wt_ref[rows, dst:dst + width] = (shifted(w_ref, rows, src, width) * scale).astype(BF16)
                return carry

            lax.fori_loop(0, d // REALIGN_ROWS, body, 0)

    hb = _rms(x_ref[0], g_ref[...]).astype(BF16)
    for c in range(0, SRC_BD, IN_CW):
        acc = _dot(hb, w_ref[:, c:c + IN_CW]) + b_ref[:, c:c + IN_CW]
        p_ref[0, :, c:c + IN_CW] = acc.astype(BF16)
    bd_ref[0] = _dot(hb, w_ref[:, SRC_BD:SRC_BD + LANES]) + b_ref[:, SRC_BD:SRC_BD + LANES]
    for c in range(0, TAIL_CK, IN_CW):
        acc = _dot(hb, wt_ref[:, c:c + IN_CW]) + bt_ref[:, c:c + IN_CW]
        p_ref[0, :, SRC_BD + c:SRC_BD + c + IN_CW] = acc.astype(BF16)
    kt = (_dot(hb, wt_ref[:, TAIL_CK:TAIL_W]) + bt_ref[:, TAIL_CK:TAIL_W]).T
    for hp in range(SB_HEADS // 2):
        for jb in range(IN_TM // SB_T):
            kt_ref[0, hp, jb] = kt[hp * LANES:(hp + 1) * LANES,
                                   jb * SB_T:(jb + 1) * SB_T].astype(BF16)


def _in_proj(x, g, w, b, layer):
    bsz, s, d = x.shape
    p_in = w.shape[-1]
    pm = SRC_BD + TAIL_CK
    n_hp = SB_HEADS // 2
    grid = (bsz, s // IN_TM)
    return pl.pallas_call(
        _in_proj_kernel,
        grid=grid,
        in_specs=[
            pl.BlockSpec((1, IN_TM, d), lambda b, i: (b, i, 0)),
            _const_spec((1, d)),
            _layer_spec((d, p_in), layer),
            _layer_spec((1, p_in), layer),
        ],
        out_specs=[
            pl.BlockSpec((1, IN_TM, pm), lambda b, i: (b, i, 0)),
            pl.BlockSpec((1, IN_TM, LANES), lambda b, i: (b, i, 0)),
            pl.BlockSpec((1, n_hp, IN_TM // SB_T, LANES, SB_T), lambda b, i: (b, 0, i, 0, 0)),
        ],
        out_shape=[
            jax.ShapeDtypeStruct((bsz, s, pm), BF16),
            jax.ShapeDtypeStruct((bsz, s, LANES), F32),
            jax.ShapeDtypeStruct((bsz, n_hp, s // SB_T, LANES, SB_T), BF16),
        ],
        scratch_shapes=[pltpu.VMEM((d, TAIL_W), BF16), pltpu.VMEM((1, TAIL_W), F32)],
        compiler_params=_cparams(("arbitrary", "arbitrary"), 60),
        name="in_proj",
    )(x, g, w, b)


SGU_T = 1024


def _sgu_kernel(u_ref, v_ref, ws_ref, bs_ref, lng_ref, lnb_ref, o_ref):
    u = _gelu(u_ref[0].astype(F32))
    v = _gelu(v_ref[0].astype(F32))
    vc = v - jnp.mean(v, axis=-1, keepdims=True)
    v = vc * lax.rsqrt(jnp.mean(vc * vc, axis=-1, keepdims=True) + NORM_EPS)
    v = (v * lng_ref[...] + lnb_ref[...]).astype(BF16)
    row = lax.broadcasted_iota(jnp.int32, (GM_CHUNK, GM_CHUNK), 0)
    col = lax.broadcasted_iota(jnp.int32, (GM_CHUNK, GM_CHUNK), 1)
    causal = col <= row
    first_half = col < (LANES // 2)
    ws = [jnp.where(causal, ws_ref[g], 0.0).astype(BF16) for g in range(GM_GROUPS)]
    for c in range(SGU_T // GM_CHUNK):
        r0 = c * GM_CHUNK
        for p in range(GM_GROUPS // 2):
            v2 = v[r0:r0 + GM_CHUNK, p * LANES:(p + 1) * LANES]
            mixed = jnp.where(first_half, _dot(ws[2 * p], v2), _dot(ws[2 * p + 1], v2))
            mixed = mixed + bs_ref[:, p * LANES:(p + 1) * LANES]
            y = u[r0:r0 + GM_CHUNK, p * LANES:(p + 1) * LANES] * mixed
            o_ref[0, r0:r0 + GM_CHUNK, p * LANES:(p + 1) * LANES] = y.astype(BF16)


def _sgu(p_main, ws, bs_full, lng, lnb):
    bsz, s, _ = p_main.shape
    w = BRANCH_W
    return pl.pallas_call(
        _sgu_kernel,
        grid=(bsz, s // SGU_T),
        in_specs=[
            pl.BlockSpec((1, SGU_T, w), lambda b, i: (b, i, COL_AU // w)),
            pl.BlockSpec((1, SGU_T, w), lambda b, i: (b, i, COL_AV // w)),
            _const_spec((GM_GROUPS, GM_CHUNK, GM_CHUNK)),
            _const_spec((GM_CHUNK, w)),
            _const_spec((1, w)),
            _const_spec((1, w)),
        ],
        out_specs=pl.BlockSpec((1, SGU_T, w), lambda b, i: (b, i, 0)),
        out_shape=jax.ShapeDtypeStruct((bsz, s, w), BF16),
        compiler_params=_cparams(("parallel", "parallel"), 32),
        name="sgu",
    )(p_main, p_main, ws, bs_full, lng, lnb)


DN_T = 256
DN_SUB = 32
HALO = 8


def _dn_kernel(q_ref, k_ref, v_ref, z_ref, qp_ref, kp_ref, vp_ref, bd_ref,
               cw_ref, alog_ref, dtb_ref, ng_ref, o_ref, state_ref):
    sb = pl.program_id(0)
    nb = q_ref.shape[0]

    @pl.when(sb == 0)
    def _():
        state_ref[...] = jnp.zeros_like(state_ref)

    not_first = (sb > 0).astype(F32)

    halo_row = lax.broadcasted_iota(jnp.int32, (HALO, BRANCH_W), 0)

    def conv_silu(x_ref, xp_ref, part, b):
        x = x_ref[b].astype(F32)
        xp = xp_ref[b].astype(F32) * not_first
        taps = cw_ref[:, part * BRANCH_W:(part + 1) * BRANCH_W]
        y = taps[CONV_W - 1:CONV_W] * x
        for sft in range(1, CONV_W):
            xr = pltpu.roll(x, sft, 0)
            head = jnp.where(halo_row < sft, pltpu.roll(xp, sft, 0), xr[:HALO])
            xs = jnp.concatenate([head, xr[HALO:]], axis=0)
            y = y + taps[CONV_W - 1 - sft:CONV_W - sft] * xs
        return y * _sigmoid(y)

    q = [conv_silu(q_ref, qp_ref, 0, b) for b in range(nb)]
    k = [conv_silu(k_ref, kp_ref, 1, b) for b in range(nb)]
    v = [conv_silu(v_ref, vp_ref, 2, b) for b in range(nb)]
    z = [z_ref[b].astype(F32) for b in range(nb)]

    lane = lax.broadcasted_iota(jnp.int32, (DN_T, LANES), 1)
    decay_lanes = (lane >= DN_HEADS) & (lane < 2 * DN_HEADS)
    beta_all, g_all = [], []
    for b in range(nb):
        bd = bd_ref[b]
        beta_all.append(_sigmoid(bd))
        g_all.append(jnp.where(decay_lanes, -jnp.exp(alog_ref[...]) * _softplus(bd + dtb_ref[...]), 0.0))

    t = DN_T
    c = DN_CHUNK
    nc = t // c
    row = lax.broadcasted_iota(jnp.int32, (t, t), 0)
    col = lax.broadcasted_iota(jnp.int32, (t, t), 1)
    same_chunk = (row // c) == (col // c)
    tri = same_chunk & (col <= row)
    strict = same_chunk & (col < row)
    same_sub = (row // DN_SUB) == (col // DN_SUB)
    eye = (col == row).astype(F32)
    col_k =lax.broadcasted_iota(jnp.int32, (DN_DIM, t), 1) // c

    l_tri = tri.astype(F32)
    gc = [jnp.dot(l_tri, g_all[b], precision=lax.Precision.HIGHEST, preferred_element_type=F32)
          for b in range(nb)]
    gct = [gc[b].T for b in range(nb)]

    heads = range(nb * DN_HEADS)
    qn, kn, vn, kb = [], [], [], []
    gcol, bcol, egc, decay, glast, k_tt = [], [], [], [], [], []
    for h in heads:
        b, hh = divmod(h, DN_HEADS)
        hs = slice(hh * DN_DIM, (hh + 1) * DN_DIM)
        qh, kh = q[b][:, hs], k[b][:, hs]
        qn.append(qh * lax.rsqrt(jnp.sum(qh * qh, axis=-1, keepdims=True) + NORM_EPS) * (DN_DIM ** -0.5))
        kn.append(kh * lax.rsqrt(jnp.sum(kh * kh, axis=-1, keepdims=True) + NORM_EPS))
        vn.append(v[b][:, hs])
        kb.append(kn[h].astype(BF16))
        gcol.append(gc[b][:, DN_HEADS + hh:DN_HEADS + hh + 1])
        grow = gct[b][DN_HEADS + hh:DN_HEADS + hh + 1, :]
        bcol.append(beta_all[b][:, hh:hh + 1])
        egc.append(jnp.exp(gcol[h]))
        decay.append(jnp.exp(jnp.where(tri, gcol[h] - grow, -1e30)))
        glast.append([gcol[h][ci * c + c - 1:ci * c + c, :] for ci in range(nc)])
        glast_rows = jnp.concatenate([jnp.broadcast_to(g, (c, 1)) for g in glast[h]], axis=0)
        k_tt.append((kn[h] * jnp.exp(glast_rows - gcol[h])).T)

    kk = [_dot_nt(kb[h], kb[h]) for h in heads]
    qk = [_dot_nt(qn[h].astype(BF16), kb[h]) for h in heads]
    qk = [jnp.where(tri, qk[h] * decay[h], 0.0) for h in heads]
    a = [jnp.where(strict, bcol[h] * kk[h] * decay[h], 0.0) for h in heads]
    a_d = [jnp.where(same_sub, a[h], 0.0) for h in heads]
    a_l = [jnp.where(same_sub, 0.0, a[h]).astype(BF16) for h in heads]
    x = [_dot(a_d[h].astype(BF16), a_d[h].astype(BF16)) for h in heads]
    t_d = [eye - a_d[h] for h in heads]
    for _ in range(DN_SUB.bit_length() - 3):
        both = [_dot(jnp.concatenate([x[h], t_d[h]], axis=0).astype(BF16), x[h].astype(BF16))
                for h in heads]
        x = [both[h][:t] for h in heads]
        t_d = [t_d[h] + both[h][t:] for h in heads]
    last = [_dot(t_d[h].astype(BF16), x[h].astype(BF16)) for h in heads]
    t_d = [(t_d[h] + last[h]).astype(BF16) for h in heads]
    rhs = [jnp.concatenate([vn[h] * bcol[h], kn[h] * (bcol[h] * egc[h])], axis=-1) for h in heads]
    sol = [_dot(t_d[h], rhs[h].astype(BF16)) for h in heads]
    for _ in range(c // DN_SUB - 1):
        below = [_dot(a_l[h], sol[h].astype(BF16)) for h in heads]
        sol = [_dot(t_d[h], (rhs[h] - below[h]).astype(BF16)) for h in heads]
    u_l = [sol[h][:, :DN_DIM] for h in heads]
    w_l = [sol[h][:, DN_DIM:] for h in heads]
    qd_l = [qn[h] * egc[h] for h in heads]

    state = [state_ref[h] for h in heads]
    for ci in range(nc):
        rs = slice(ci * c, (ci + 1) * c)
        r1 = [_dot(jnp.concatenate([w_l[h][rs], qd_l[h][rs]], axis=0).astype(BF16), state[h].astype(BF16))
              for h in heads]
        lhs2, v_pad = [], []
        for h in heads:
            pieces = []
            if ci > 0:
                pieces.append(jnp.zeros((ci * c, DN_DIM), F32))
            pieces.append(u_l[h][rs] - r1[h][:c])
            if ci < nc - 1:
                pieces.append(jnp.zeros(((nc - 1 - ci) * c, DN_DIM), F32))
            v_pad.append(jnp.concatenate(pieces, axis=0).astype(BF16))
            ktt_c = jnp.where(col_k == ci, k_tt[h], jnp.zeros_like(k_tt[h]))
            lhs2.append(jnp.concatenate([qk[h][rs, :], ktt_c], axis=0).astype(BF16))
        r2 = [_dot(lhs2[h], v_pad[h]) for h in heads]
        for h in heads:
            b, hh = divmod(h, DN_HEADS)
            hs = slice(hh * DN_DIM, (hh + 1) * DN_DIM)
            state[h] = state[h] * jnp.exp(glast[h][ci]) + r2[h][c:]
            o = _rms(r1[h][c:] + r2[h][:c], ng_ref[...])
            zh = z[b][rs, hs]
            o_ref[b, rs, hs] = (o * (zh * _sigmoid(zh))).astype(BF16)
    for h in heads:
        state_ref[h] = state[h]


def _deltanet(p_main, bd, conv_w, alog_l, dtb_l, ng):
    bsz, s, _ = p_main.shape
    w = BRANCH_W
    hb = DN_T // HALO

    def cur(col):
        return pl.BlockSpec((bsz, DN_T, w), lambda i: (0, i, col // w))

    def prev(col):
        return pl.BlockSpec((bsz, HALO, w), lambda i: (0, jnp.maximum(i * hb - 1, 0), col // w))

    return pl.pallas_call(
        _dn_kernel,
        grid=(s // DN_T,),
        in_specs=[
            cur(COL_BQ), cur(COL_BK), cur(COL_BV), cur(COL_BZ),
            prev(COL_BQ), prev(COL_BK), prev(COL_BV),
            pl.BlockSpec((bsz, DN_T, LANES), lambda i: (0, i, 0)),
            _const_spec((CONV_W, 3 * w)),
            _const_spec((1, LANES)),
            _const_spec((1, LANES)),
            _const_spec((1, DN_DIM)),
        ],
        out_specs=pl.BlockSpec((bsz, DN_T, w), lambda i: (0, i, 0)),
        out_shape=jax.ShapeDtypeStruct((bsz, s, w), BF16),
        scratch_shapes=[pltpu.VMEM((bsz * DN_HEADS, DN_DIM, DN_DIM), F32)],
        compiler_params=_cparams(("arbitrary",), 48),
        name="deltanet",
    )(p_main, p_main, p_main, p_main, p_main, p_main, p_main, bd, conv_w, alog_l, dtb_l, ng)


def _sb_kernel(q_ref, kt_ref, v_ref, o_ref):
    i = pl.program_id(2)
    t = SB_T
    pairs = range(SB_GROUP)
    lane = lax.broadcasted_iota(jnp.int32, (t, LANES), 1)
    head0 = lane < SB_DIM
    qs = []
    for p in pairs:
        q2 = q_ref[0, :, p * LANES:(p + 1) * LANES]
        zq = jnp.zeros_like(q2)
        qs.append(jnp.concatenate([jnp.where(head0, q2, zq), jnp.where(head0, zq, q2)], axis=0))
    row = lax.broadcasted_iota(jnp.int32, (t, t), 0)
    col = lax.broadcasted_iota(jnp.int32, (t, t), 1)
    after = (row > col).astype(BF16)
    causal = jnp.concatenate([col < row, col < row], axis=0)

    def sweep(j_lo, n, mask_last, carry, acc):
        rows = pl.ds(pl.multiple_of(j_lo * t, t), n * t)
        z = [_dot(qs[p], jnp.concatenate([kt_ref[0, p, j_lo + m] for m in range(n)], axis=1))
             for p in pairs]
        log_keep, log_beta = [], []
        for p in pairs:
            lks, lbs = [], []
            for m in range(n):
                y = z[p][:, m * t:(m + 1) * t] * LOG2E
                ny = -y
                lk = jnp.minimum(ny, 0.0) - jnp.log2(1.0 + jnp.exp2(jnp.minimum(y, ny)))
                lbs.append(y + lk)
                if m == n - 1 and mask_last is not None:
                    lk = jnp.where(mask_last, lk, 0.0)
                lks.append(lk)
            log_keep.append(lks)
            log_beta.append(lbs)
        later = [_dot(jnp.concatenate(log_keep[p], axis=0).astype(BF16), after)
                 for p in pairs]
        w, new_carry = [], []
        for p in pairs:
            cp = carry[p]
            ws = [None] * n
            for m in reversed(range(n)):
                wp = jnp.exp2(log_beta[p][m] + later[p][m * 2 * t:(m + 1) * 2 * t] + cp)
                if m == n - 1 and mask_last is not None:
                    wp = jnp.where(mask_last, wp, 0.0)
                ws[m] = wp
                cp = cp + jnp.sum(log_keep[p][m], axis=1, keepdims=True)
            w.append(jnp.concatenate(ws, axis=1).astype(BF16))
            new_carry.append(cp)
        pv = [_dot(w[p], v_ref[0, rows, p * LANES:(p + 1) * LANES]) for p in pairs]
        return tuple(new_carry), tuple(acc[p] + pv[p] for p in pairs)

    def write(acc):
        for p in pairs:
            o_ref[0, :, p * LANES:(p + 1) * LANES] = jnp.where(head0, acc[p][:t], acc[p][t:]).astype(BF16)

    zero_carry = (jnp.zeros((2 * t, 1), F32),) * SB_GROUP
    zero_acc = (jnp.zeros((2 * t, LANES), F32),) * SB_GROUP

    @pl.when(i == 0)
    def _():
        write(sweep(0, 1, causal, zero_carry, zero_acc)[1])

    @pl.when(i > 0)
    def _():
        carry, acc = sweep(i - 1, 2, causal, zero_carry, zero_acc)

        def cond(st):
            j, carry, _ = st
            top = functools.reduce(jnp.maximum, [jnp.max(cp) for cp in carry])
            return jnp.logical_and(j >= 0, top > -SB_SKIP)

        def body(st):
            j, carry, acc = st
            carry, acc = sweep(j, 1, None, carry, acc)
            return j - 1, carry, acc

        write(lax.while_loop(cond, body, (i - 2, carry, acc))[2])


def _sb_attn(p_main, kt):
    bsz, s, _ = p_main.shape
    n_grp = SB_HEADS // 2 // SB_GROUP
    gw = SB_GROUP * LANES
    nb = s // SB_T
    return pl.pallas_call(
        _sb_kernel,
        grid=(bsz, n_grp, nb),
        in_specs=[
            pl.BlockSpec((1, SB_T, gw), lambda b, h, i: (b, i, COL_CQ // gw + h)),
            pl.BlockSpec((1, SB_GROUP, nb, LANES, SB_T), lambda b, h, i: (b, h, 0, 0, 0),
                         pipeline_mode=pl.Buffered(1)),
            pl.BlockSpec((1, s, gw), lambda b, h, i: (b, 0, COL_CV // gw + h), pipeline_mode=pl.Buffered(1)),
        ],
        out_specs=pl.BlockSpec((1, SB_T, gw), lambda b, h, i: (b, i, h)),
        out_shape=jax.ShapeDtypeStruct((bsz, s, BRANCH_W), BF16),
        compiler_params=_cparams(("parallel", "parallel", "arbitrary"), 48),
        name="sb_attn",
    )(p_main, kt, p_main)


MG_TM = 512


def _merge_kernel(x_ref, ga_ref, gb_ref, gc_ref, ya_ref, yb_ref, yc_ref, wb_ref, wo_ref, ng_ref, o_ref):
    m = None
    for n, (g_ref, y_ref) in enumerate(((ga_ref, ya_ref), (gb_ref, yb_ref), (gc_ref, yc_ref))):
        proj = _dot(y_ref[0], wb_ref[n])
        gate = _sigmoid(g_ref[0].astype(F32))
        m = gate * proj if m is None else m + gate * proj
    mixed = _dot(m.astype(BF16), wo_ref[...])
    o_ref[0] = x_ref[0] + _rms(mixed, ng_ref[...])


FF_CW = 1024


def _ffn_kernel(x_ref, g1_ref, w1_ref, w2_ref, g2_ref, o_ref):
    x = x_ref[0]
    hb = _rms(x, g1_ref[...]).astype(BF16)
    dff = w1_ref.shape[1]
    f = None
    for c in range(0, dff, FF_CW):
        a = jnp.maximum(_dot(hb, w1_ref[:, c:c + FF_CW]), 0.0)
        part = _dot((a * a).astype(BF16), w2_ref[c:c + FF_CW, :])
        f = part if f is None else f + part
    o_ref[0] = x + _rms(f, g2_ref[...])


def _merge_ffn_kernel(x_ref, ga_ref, gb_ref, gc_ref, ya_ref, yb_ref, yc_ref, wb_ref, wo_ref, ng_ref,
                      g1_ref, w1_ref, w2_ref, g2_ref, o_ref, xm_ref):
    _merge_kernel(x_ref, ga_ref, gb_ref, gc_ref, ya_ref, yb_ref, yc_ref, wb_ref, wo_ref, ng_ref, xm_ref)
    _ffn_kernel(xm_ref, g1_ref, w1_ref, w2_ref, g2_ref, o_ref)


def _merge_ffn(x, p_main, ya, yb, yc, wb, wo, ng, g1, w1, w2, g2, layer):
    bsz, s, d = x.shape
    w = BRANCH_W
    dff = w1.shape[-1]
    y_spec = pl.BlockSpec((1, MG_TM, w), lambda b, i: (b, i, 0))

    def gate_spec(n):
        return pl.BlockSpec((1, MG_TM, d), lambda b, i: (b, i, COL_GATE // d + n))

    return pl.pallas_call(
        _merge_ffn_kernel,
        grid=(bsz, s // MG_TM),
        in_specs=[
            pl.BlockSpec((1, MG_TM, d), lambda b, i: (b, i, 0)),
            gate_spec(0), gate_spec(1), gate_spec(2),
            y_spec, y_spec, y_spec,
            _layer_spec((N_BRANCH, w, d), layer),
            _layer_spec((d, d), layer),
            _const_spec((1, d)),
            _const_spec((1, d)),
            _layer_spec((d, dff), layer),
            _layer_spec((dff, d), layer),
            _const_spec((1, d)),
        ],
        out_specs=pl.BlockSpec((1, MG_TM, d), lambda b, i: (b, i, 0)),
        out_shape=jax.ShapeDtypeStruct((bsz, s, d), F32),
        scratch_shapes=[pltpu.VMEM((1, MG_TM, d), F32)],
        compiler_params=_cparams(("parallel", "parallel"), 60),
        name="merge_ffn",
    )(x, p_main, p_main, p_main, ya, yb, yc, wb, wo, ng, g1, w1, w2, g2)


def _lane_row(vals, start):
    return jnp.zeros((1, LANES), F32).at[0, start:start + vals.shape[0]].set(vals.astype(F32))


def kernel(x, norm_g, w_in, b_in, sgu_ln_g, sgu_ln_b, w_spatial, b_spatial, conv_w, a_log, dt_bias,
           dn_norm_g, w_branch, w_out, w_ff1, w_ff2):
    depth = norm_g.shape[0]
    x = x.astype(F32)
    lane_pad = -w_in.shape[-1] % LANES
    w_in_b = jnp.pad(w_in, ((0, 0), (0, 0), (0, lane_pad))).astype(BF16)
    b_in_p = jnp.pad(b_in.astype(F32), ((0, 0), (0, lane_pad)))[:, None, :]
    w_branch_b, w_out_b = w_branch.astype(BF16), w_out.astype(BF16)
    w_ff1_b, w_ff2_b = w_ff1.astype(BF16), w_ff2.astype(BF16)
    for l in range(depth):
        p_main, bd, kt = _in_proj(x, norm_g[l, 0][None, :], w_in_b, b_in_p, l)

        bs_full = jnp.repeat(b_spatial[l].T, BRANCH_W // GM_GROUPS, axis=1)
        y_a = _sgu(p_main, w_spatial[l], bs_full, sgu_ln_g[l][None, :], sgu_ln_b[l][None, :])

        y_b = _deltanet(p_main, bd, conv_w[l], _lane_row(a_log[l], DN_HEADS),
                        _lane_row(dt_bias[l], DN_HEADS), dn_norm_g[l][None, :])

        y_c = _sb_attn(p_main, kt)

        x = _merge_ffn(x, p_main, y_a, y_b, y_c, w_branch_b, w_out_b, norm_g[l, 1][None, :],
                       norm_g[l, 2][None, :], w_ff1_b, w_ff2_b, norm_g[l, 3][None, :], l)
    return x
```

```python
import functools

import jax
import jax.numpy as jnp
from jax import lax
from jax.experimental import pallas as pl
from jax.experimental.pallas import tpu as pltpu

F32 = jnp.float32
BF16 = jnp.bfloat16
NORM_EPS = 1e-6

LANES = 128
MIB = 1024 * 1024

BRANCH_W = 512
GM_CHUNK = 128
GM_GROUPS = 8
DN_HEADS = 4
DN_DIM = 128
DN_CHUNK = 64
CONV_W = 4
SB_HEADS = 8
SB_DIM = 64
N_BRANCH = 3

COL_AU, COL_AV, COL_BQ, COL_BK, COL_BV, COL_BZ, COL_CQ, COL_CV, COL_GATE = (
    0, 512, 1024, 1536, 2048, 2560, 3072, 3584, 4096)


SB_T = 256
LOG2E = 1.4426950408889634
SB_SKIP = 100.0 * LOG2E
SB_GROUP = 4


def _cparams(sem, vmem_mib):
    return pltpu.CompilerParams(dimension_semantics=sem, vmem_limit_bytes=vmem_mib * MIB)


def _const_spec(shape):
    nd = len(shape)
    return pl.BlockSpec(shape, lambda *_: (0,) * nd, pipeline_mode=pl.Buffered(1))


def _layer_spec(shape, layer):
    nd = len(shape)
    return pl.BlockSpec((None,) + tuple(shape), lambda *_: (layer,) + (0,) * nd, pipeline_mode=pl.Buffered(1))


def _rms(x, g):
    return x * lax.rsqrt(jnp.mean(x * x, axis=-1, keepdims=True) + NORM_EPS) * g


def _sigmoid(x):
    return 1.0 / (1.0 + jnp.exp(-x))


def _softplus(x):
    return jnp.maximum(x, 0.0) + jnp.log(1.0 + jnp.exp(-jnp.abs(x)))


def _gelu(x):
    return 0.5 * x * (1.0 + lax.erf(x * (2.0 ** -0.5)))


def _dot(a, b):
    return jnp.dot(a, b, preferred_element_type=F32)


def _dot_nt(a, b):
    return lax.dot_general(a, b, (((1,), (1,)), ((), ())), preferred_element_type=F32)


IN_TM = 512
IN_CW = 1024


SRC_BD = 6 * BRANCH_W
SRC_CQ = SRC_BD + 2 * DN_HEADS
SRC_CK = SRC_CQ + BRANCH_W
SRC_CV = SRC_CK + BRANCH_W
SRC_GATE = SRC_CV + BRANCH_W
TAIL_CK = COL_GATE - SRC_BD + N_BRANCH * 2 * BRANCH_W
TAIL_W = TAIL_CK + BRANCH_W
REALIGN_ROWS = 128


def _in_proj_kernel(x_ref, g_ref, w_ref, b_ref, p_ref, bd_ref, kt_ref, wt_ref, bt_ref):
    d = w_ref.shape[0]
    gate_w = N_BRANCH * 2 * BRANCH_W
    moves = ((SRC_CQ, 0, BRANCH_W, SB_DIM ** -0.5), (SRC_CV, BRANCH_W, BRANCH_W, 1.0),
             (SRC_GATE, COL_GATE - SRC_BD, gate_w, 1.0), (SRC_CK, TAIL_CK, BRANCH_W, 1.0))

    def shifted(ref, rows, src, width):
        lo = src // LANES * LANES
        hi = min(-(-(src + width) // LANES) * LANES, ref.shape[-1])
        return ref[rows, lo:hi].astype(F32)[:, src - lo:src - lo + width]

    @pl.when((pl.program_id(0) == 0) & (pl.program_id(1) == 0))
    def _():
        for src, dst, width, scale in moves:
            bt_ref[:, dst:dst + width] = shifted(b_ref, slice(None), src, width) * scale

            def body(r, carry, src=src, dst=dst, width=width, scale=scale):
                rows = pl.ds(pl.multiple_of(r * REALIGN_ROWS, REALIGN_ROWS), REALIGN_ROWS)
                wt_ref[rows, dst:dst + width] = (shifted(w_ref, rows, src, width) * scale).astype(BF16)
                return carry

            lax.fori_loop(0, d // REALIGN_ROWS, body, 0)

    hb = _rms(x_ref[0], g_ref[...]).astype(BF16)
    for c in range(0, SRC_BD, IN_CW):
        acc = _dot(hb, w_ref[:, c:c + IN_CW]) + b_ref[:, c:c + IN_CW]
        p_ref[0, :, c:c + IN_CW] = acc.astype(BF16)
    bd_ref[0] = _dot(hb, w_ref[:, SRC_BD:SRC_BD + LANES]) + b_ref[:, SRC_BD:SRC_BD + LANES]
    for c in range(0, TAIL_CK, IN_CW):
        acc = _dot(hb, wt_ref[:, c:c + IN_CW]) + bt_ref[:, c:c + IN_CW]
        p_ref[0, :, SRC_BD + c:SRC_BD + c + IN_CW] = acc.astype(BF16)
    kt = (_dot(hb, wt_ref[:, TAIL_CK:TAIL_W]) + bt_ref[:, TAIL_CK:TAIL_W]).T
    for hp in range(SB_HEADS // 2):
        for jb in range(IN_TM // SB_T):
            kt_ref[0, hp, jb] = kt[hp * LANES:(hp + 1) * LANES,
                                   jb * SB_T:(jb + 1) * SB_T].astype(BF16)


def _in_proj(x, g, w, b, layer):
    bsz, s, d = x.shape
    p_in = w.shape[-1]
    pm = SRC_BD + TAIL_CK
    n_hp = SB_HEADS // 2
    grid = (bsz, s // IN_TM)
    return pl.pallas_call(
        _in_proj_kernel,
        grid=grid,
        in_specs=[
            pl.BlockSpec((1, IN_TM, d), lambda b, i: (b, i, 0)),
            _const_spec((1, d)),
            _layer_spec((d, p_in), layer),
            _layer_spec((1, p_in), layer),
        ],
        out_specs=[
            pl.BlockSpec((1, IN_TM, pm), lambda b, i: (b, i, 0)),
            pl.BlockSpec((1, IN_TM, LANES), lambda b, i: (b, i, 0)),
            pl.BlockSpec((1, n_hp, IN_TM // SB_T, LANES, SB_T), lambda b, i: (b, 0, i, 0, 0)),
        ],
        out_shape=[
            jax.ShapeDtypeStruct((bsz, s, pm), BF16),
            jax.ShapeDtypeStruct((bsz, s, LANES), F32),
            jax.ShapeDtypeStruct((bsz, n_hp, s // SB_T, LANES, SB_T), BF16),
        ],
        scratch_shapes=[pltpu.VMEM((d, TAIL_W), BF16), pltpu.VMEM((1, TAIL_W), F32)],
        compiler_params=_cparams(("arbitrary", "arbitrary"), 60),
        name="in_proj",
    )(x, g, w, b)


SGU_T = 1024


def _sgu_kernel(u_ref, v_ref, ws_ref, bs_ref, lng_ref, lnb_ref, o_ref):
    u = _gelu(u_ref[0].astype(F32))
    v = _gelu(v_ref[0].astype(F32))
    vc = v - jnp.mean(v, axis=-1, keepdims=True)
    v = vc * lax.rsqrt(jnp.mean(vc * vc, axis=-1, keepdims=True) + NORM_EPS)
    v = (v * lng_ref[...] + lnb_ref[...]).astype(BF16)
    row = lax.broadcasted_iota(jnp.int32, (GM_CHUNK, GM_CHUNK), 0)
    col = lax.broadcasted_iota(jnp.int32, (GM_CHUNK, GM_CHUNK), 1)
    causal = col <= row
    first_half = col < (LANES // 2)
    ws = [jnp.where(causal, ws_ref[g], 0.0).astype(BF16) for g in range(GM_GROUPS)]
    for c in range(SGU_T // GM_CHUNK):
        r0 = c * GM_CHUNK
        for p in range(GM_GROUPS // 2):
            v2 = v[r0:r0 + GM_CHUNK, p * LANES:(p + 1) * LANES]
            mixed = jnp.where(first_half, _dot(ws[2 * p], v2), _dot(ws[2 * p + 1], v2))
            mixed = mixed + bs_ref[:, p * LANES:(p + 1) * LANES]
            y = u[r0:r0 + GM_CHUNK, p * LANES:(p + 1) * LANES] * mixed
            o_ref[0, r0:r0 + GM_CHUNK, p * LANES:(p + 1) * LANES] = y.astype(BF16)


def _sgu(p_main, ws, bs_full, lng, lnb):
    bsz, s, _ = p_main.shape
    w = BRANCH_W
    return pl.pallas_call(
        _sgu_kernel,
        grid=(bsz, s // SGU_T),
        in_specs=[
            pl.BlockSpec((1, SGU_T, w), lambda b, i: (b, i, COL_AU // w)),
            pl.BlockSpec((1, SGU_T, w), lambda b, i: (b, i, COL_AV // w)),
            _const_spec((GM_GROUPS, GM_CHUNK, GM_CHUNK)),
            _const_spec((GM_CHUNK, w)),
            _const_spec((1, w)),
            _const_spec((1, w)),
        ],
        out_specs=pl.BlockSpec((1, SGU_T, w), lambda b, i: (b, i, 0)),
        out_shape=jax.ShapeDtypeStruct((bsz, s, w), BF16),
        compiler_params=_cparams(("parallel", "parallel"), 32),
        name="sgu",
    )(p_main, p_main, ws, bs_full, lng, lnb)


DN_T = 256
DN_SUB = 32
HALO = 8


def _dn_kernel(q_ref, k_ref, v_ref, z_ref, qp_ref, kp_ref, vp_ref, bd_ref,
               cw_ref, alog_ref, dtb_ref, ng_ref, o_ref, state_ref):
    sb = pl.program_id(0)
    nb = q_ref.shape[0]

    @pl.when(sb == 0)
    def _():
        state_ref[...] = jnp.zeros_like(state_ref)

    not_first = (sb > 0).astype(F32)

    halo_row = lax.broadcasted_iota(jnp.int32, (HALO, BRANCH_W), 0)

    def conv_silu(x_ref, xp_ref, part, b):
        x = x_ref[b].astype(F32)
        xp = xp_ref[b].astype(F32) * not_first
        taps = cw_ref[:, part * BRANCH_W:(part + 1) * BRANCH_W]
        y = taps[CONV_W - 1:CONV_W] * x
        for sft in range(1, CONV_W):
            xr = pltpu.roll(x, sft, 0)
            head = jnp.where(halo_row < sft, pltpu.roll(xp, sft, 0), xr[:HALO])
            xs = jnp.concatenate([head, xr[HALO:]], axis=0)
            y = y + taps[CONV_W - 1 - sft:CONV_W - sft] * xs
        return y * _sigmoid(y)

    q = [conv_silu(q_ref, qp_ref, 0, b) for b in range(nb)]
    k = [conv_silu(k_ref, kp_ref, 1, b) for b in range(nb)]
    v = [conv_silu(v_ref, vp_ref, 2, b) for b in range(nb)]
    z = [z_ref[b].astype(F32) for b in range(nb)]

    lane = lax.broadcasted_iota(jnp.int32, (DN_T, LANES), 1)
    decay_lanes = (lane >= DN_HEADS) & (lane < 2 * DN_HEADS)
    beta_all, g_all = [], []
    for b in range(nb):
        bd = bd_ref[b]
        beta_all.append(_sigmoid(bd))
        g_all.append(jnp.where(decay_lanes, -jnp.exp(alog_ref[...]) * _softplus(bd + dtb_ref[...]), 0.0))

    t = DN_T
    c = DN_CHUNK
    nc = t // c
    row = lax.broadcasted_iota(jnp.int32, (t, t), 0)
    col = lax.broadcasted_iota(jnp.int32, (t, t), 1)
    same_chunk = (row // c) == (col // c)
    tri = same_chunk & (col <= row)
    strict = same_chunk & (col < row)
    same_sub = (row // DN_SUB) == (col // DN_SUB)
    eye = (col == row).astype(F32)
    col_k =lax.broadcasted_iota(jnp.int32, (DN_DIM, t), 1) // c

    l_tri = tri.astype(F32)
    gc = [jnp.dot(l_tri, g_all[b], precision=lax.Precision.HIGHEST, preferred_element_type=F32)
          for b in range(nb)]
    gct = [gc[b].T for b in range(nb)]

    heads = range(nb * DN_HEADS)
    qn, kn, vn, kb = [], [], [], []
    gcol, bcol, egc, decay, glast, k_tt = [], [], [], [], [], []
    for h in heads:
        b, hh = divmod(h, DN_HEADS)
        hs = slice(hh * DN_DIM, (hh + 1) * DN_DIM)
        qh, kh = q[b][:, hs], k[b][:, hs]
        qn.append(qh * lax.rsqrt(jnp.sum(qh * qh, axis=-1, keepdims=True) + NORM_EPS) * (DN_DIM ** -0.5))
        kn.append(kh * lax.rsqrt(jnp.sum(kh * kh, axis=-1, keepdims=True) + NORM_EPS))
        vn.append(v[b][:, hs])
        kb.append(kn[h].astype(BF16))
        gcol.append(gc[b][:, DN_HEADS + hh:DN_HEADS + hh + 1])
        grow = gct[b][DN_HEADS + hh:DN_HEADS + hh + 1, :]
        bcol.append(beta_all[b][:, hh:hh + 1])
        egc.append(jnp.exp(gcol[h]))
        decay.append(jnp.exp(jnp.where(tri, gcol[h] - grow, -1e30)))
        glast.append([gcol[h][ci * c + c - 1:ci * c + c, :] for ci in range(nc)])
        glast_rows = jnp.concatenate([jnp.broadcast_to(g, (c, 1)) for g in glast[h]], axis=0)
        k_tt.append((kn[h] * jnp.exp(glast_rows - gcol[h])).T)

    kk = [_dot_nt(kb[h], kb[h]) for h in heads]
    qk = [_dot_nt(qn[h].astype(BF16), kb[h]) for h in heads]
    qk = [jnp.where(tri, qk[h] * decay[h], 0.0) for h in heads]
    a = [jnp.where(strict, bcol[h] * kk[h] * decay[h], 0.0) for h in heads]
    a_d = [jnp.where(same_sub, a[h], 0.0) for h in heads]
    a_l = [jnp.where(same_sub, 0.0, a[h]).astype(BF16) for h in heads]
    x = [_dot(a_d[h].astype(BF16), a_d[h].astype(BF16)) for h in heads]
    t_d = [eye - a_d[h] for h in heads]
    for _ in range(DN_SUB.bit_length() - 3):
        both = [_dot(jnp.concatenate([x[h], t_d[h]], axis=0).astype(BF16), x[h].astype(BF16))
                for h in heads]
        x = [both[h][:t] for h in heads]
        t_d = [t_d[h] + both[h][t:] for h in heads]
    last = [_dot(t_d[h].astype(BF16), x[h].astype(BF16)) for h in heads]
    t_d = [(t_d[h] + last[h]).astype(BF16) for h in heads]
    rhs = [jnp.concatenate([vn[h] * bcol[h], kn[h] * (bcol[h] * egc[h])], axis=-1) for h in heads]
    sol = [_dot(t_d[h], rhs[h].astype(BF16)) for h in heads]
    for _ in range(c // DN_SUB - 1):
        below = [_dot(a_l[h], sol[h].astype(BF16)) for h in heads]
        sol = [_dot(t_d[h], (rhs[h] - below[h]).astype(BF16)) for h in heads]
    u_l = [sol[h][:, :DN_DIM] for h in heads]
    w_l = [sol[h][:, DN_DIM:] for h in heads]
    qd_l = [qn[h] * egc[h] for h in heads]

    state = [state_ref[h] for h in heads]
    for ci in range(nc):
        rs = slice(ci * c, (ci + 1) * c)
        r1 = [_dot(jnp.concatenate([w_l[h][rs], qd_l[h][rs]], axis=0).astype(BF16), state[h].astype(BF16))
              for h in heads]
        lhs2, v_pad = [], []
        for h in heads:
            pieces = []
            if ci > 0:
                pieces.append(jnp.zeros((ci * c, DN_DIM), F32))
            pieces.append(u_l[h][rs] - r1[h][:c])
            if ci < nc - 1:
                pieces.append(jnp.zeros(((nc - 1 - ci) * c, DN_DIM), F32))
            v_pad.append(jnp.concatenate(pieces, axis=0).astype(BF16))
            ktt_c = jnp.where(col_k == ci, k_tt[h], jnp.zeros_like(k_tt[h]))
            lhs2.append(jnp.concatenate([qk[h][rs, :], ktt_c], axis=0).astype(BF16))
        r2 = [_dot(lhs2[h], v_pad[h]) for h in heads]
        for h in heads:
            b, hh = divmod(h, DN_HEADS)
            hs = slice(hh * DN_DIM, (hh + 1) * DN_DIM)
            state[h] = state[h] * jnp.exp(glast[h][ci]) + r2[h][c:]
            o = _rms(r1[h][c:] + r2[h][:c], ng_ref[...])
            zh = z[b][rs, hs]
            o_ref[b, rs, hs] = (o * (zh * _sigmoid(zh))).astype(BF16)
    for h in heads:
        state_ref[h] = state[h]


def _deltanet(p_main, bd, conv_w, alog_l, dtb_l, ng):
    bsz, s, _ = p_main.shape
    w = BRANCH_W
    hb = DN_T // HALO

    def cur(col):
        return pl.BlockSpec((bsz, DN_T, w), lambda i: (0, i, col // w))

    def prev(col):
        return pl.BlockSpec((bsz, HALO, w), lambda i: (0, jnp.maximum(i * hb - 1, 0), col // w))

    return pl.pallas_call(
        _dn_kernel,
        grid=(s // DN_T,),
        in_specs=[
            cur(COL_BQ), cur(COL_BK), cur(COL_BV), cur(COL_BZ),
            prev(COL_BQ), prev(COL_BK), prev(COL_BV),
            pl.BlockSpec((bsz, DN_T, LANES), lambda i: (0, i, 0)),
            _const_spec((CONV_W, 3 * w)),
            _const_spec((1, LANES)),
            _const_spec((1, LANES)),
            _const_spec((1, DN_DIM)),
        ],
        out_specs=pl.BlockSpec((bsz, DN_T, w), lambda i: (0, i, 0)),
        out_shape=jax.ShapeDtypeStruct((bsz, s, w), BF16),
        scratch_shapes=[pltpu.VMEM((bsz * DN_HEADS, DN_DIM, DN_DIM), F32)],
        compiler_params=_cparams(("arbitrary",), 48),
        name="deltanet",
    )(p_main, p_main, p_main, p_main, p_main, p_main, p_main, bd, conv_w, alog_l, dtb_l, ng)


def _sb_kernel(q_ref, kt_ref, v_ref, o_ref):
    i = pl.program_id(2)
    t = SB_T
    pairs = range(SB_GROUP)
    lane = lax.broadcasted_iota(jnp.int32, (t, LANES), 1)
    head0 = lane < SB_DIM
    qs = []
    for p in pairs:
        q2 = q_ref[0, :, p * LANES:(p + 1) * LANES]
        zq = jnp.zeros_like(q2)
        qs.append(jnp.concatenate([jnp.where(head0, q2, zq), jnp.where(head0, zq, q2)], axis=0))
    row = lax.broadcasted_iota(jnp.int32, (t, t), 0)
    col = lax.broadcasted_iota(jnp.int32, (t, t), 1)
    after = (row > col).astype(BF16)
    causal = jnp.concatenate([col < row, col < row], axis=0)

    def sweep(j_lo, n, mask_last, carry, acc):
        rows = pl.ds(pl.multiple_of(j_lo * t, t), n * t)
        z = [_dot(qs[p], jnp.concatenate([kt_ref[0, p, j_lo + m] for m in range(n)], axis=1))
             for p in pairs]
        log_keep, log_beta = [], []
        for p in pairs:
            lks, lbs = [], []
            for m in range(n):
                y = z[p][:, m * t:(m + 1) * t] * LOG2E
                ny = -y
                lk = jnp.minimum(ny, 0.0) - jnp.log2(1.0 + jnp.exp2(jnp.minimum(y, ny)))
                lbs.append(y + lk)
                if m == n - 1 and mask_last is not None:
                    lk = jnp.where(mask_last, lk, 0.0)
                lks.append(lk)
            log_keep.append(lks)
            log_beta.append(lbs)
        later = [_dot(jnp.concatenate(log_keep[p], axis=0).astype(BF16), after)
                 for p in pairs]
        w, new_carry = [], []
        for p in pairs:
            cp = carry[p]
            ws = [None] * n
            for m in reversed(range(n)):
                wp = jnp.exp2(log_beta[p][m] + later[p][m * 2 * t:(m + 1) * 2 * t] + cp)
                if m == n - 1 and mask_last is not None:
                    wp = jnp.where(mask_last, wp, 0.0)
                ws[m] = wp
                cp = cp + jnp.sum(log_keep[p][m], axis=1, keepdims=True)
            w.append(jnp.concatenate(ws, axis=1).astype(BF16))
            new_carry.append(cp)
        pv = [_dot(w[p], v_ref[0, rows, p * LANES:(p + 1) * LANES]) for p in pairs]
        return tuple(new_carry), tuple(acc[p] + pv[p] for p in pairs)

    def write(acc):
        for p in pairs:
            o_ref[0, :, p * LANES:(p + 1) * LANES] = jnp.where(head0, acc[p][:t], acc[p][t:]).astype(BF16)

    zero_carry = (jnp.zeros((2 * t, 1), F32),) * SB_GROUP
    zero_acc = (jnp.zeros((2 * t, LANES), F32),) * SB_GROUP

    @pl.when(i == 0)
    def _():
        write(sweep(0, 1, causal, zero_carry, zero_acc)[1])

    @pl.when(i > 0)
    def _():
        carry, acc = sweep(i - 1, 2, causal, zero_carry, zero_acc)

        def cond(st):
            j, carry, _ = st
            top = functools.reduce(jnp.maximum, [jnp.max(cp) for cp in carry])
            return jnp.logical_and(j >= 0, top > -SB_SKIP)

        def body(st):
            j, carry, acc = st
            carry, acc = sweep(j, 1, None, carry, acc)
            return j - 1, carry, acc

        write(lax.while_loop(cond, body, (i - 2, carry, acc))[2])


def _sb_attn(p_main, kt):
    bsz, s, _ = p_main.shape
    n_grp = SB_HEADS // 2 // SB_GROUP
    gw = SB_GROUP * LANES
    nb = s // SB_T
    return pl.pallas_call(
        _sb_kernel,
        grid=(bsz, n_grp, nb),
        in_specs=[
            pl.BlockSpec((1, SB_T, gw), lambda b, h, i: (b, i, COL_CQ // gw + h)),
            pl.BlockSpec((1, SB_GROUP, nb, LANES, SB_T), lambda b, h, i: (b, h, 0, 0, 0),
                         pipeline_mode=pl.Buffered(1)),
            pl.BlockSpec((1, s, gw), lambda b, h, i: (b, 0, COL_CV // gw + h), pipeline_mode=pl.Buffered(1)),
        ],
        out_specs=pl.BlockSpec((1, SB_T, gw), lambda b, h, i: (b, i, h)),
        out_shape=jax.ShapeDtypeStruct((bsz, s, BRANCH_W), BF16),
        compiler_params=_cparams(("parallel", "parallel", "arbitrary"), 48),
        name="sb_attn",
    )(p_main, kt, p_main)


MG_TM = 512


def _merge_kernel(x_ref, ga_ref, gb_ref, gc_ref, ya_ref, yb_ref, yc_ref, wb_ref, wo_ref, ng_ref, o_ref):
    m = None
    for n, (g_ref, y_ref) in enumerate(((ga_ref, ya_ref), (gb_ref, yb_ref), (gc_ref, yc_ref))):
        proj = _dot(y_ref[0], wb_ref[n])
        gate = _sigmoid(g_ref[0].astype(F32))
        m = gate * proj if m is None else m + gate * proj
    mixed = _dot(m.astype(BF16), wo_ref[...])
    o_ref[0] = x_ref[0] + _rms(mixed, ng_ref[...])


FF_CW = 1024


def _ffn_kernel(x_ref, g1_ref, w1_ref, w2_ref, g2_ref, o_ref):
    x = x_ref[0]
    hb = _rms(x, g1_ref[...]).astype(BF16)
    dff = w1_ref.shape[1]
    f = None
    for c in range(0, dff, FF_CW):
        a = jnp.maximum(_dot(hb, w1_ref[:, c:c + FF_CW]), 0.0)
        part = _dot((a * a).astype(BF16), w2_ref[c:c + FF_CW, :])
        f = part if f is None else f + part
    o_ref[0] = x + _rms(f, g2_ref[...])


def _merge_ffn_kernel(x_ref, ga_ref, gb_ref, gc_ref, ya_ref, yb_ref, yc_ref, wb_ref, wo_ref, ng_ref,
                      g1_ref, w1_ref, w2_ref, g2_ref, o_ref, xm_ref):
    _merge_kernel(x_ref, ga_ref, gb_ref, gc_ref, ya_ref, yb_ref, yc_ref, wb_ref, wo_ref, ng_ref, xm_ref)
    _ffn_kernel(xm_ref, g1_ref, w1_ref, w2_ref, g2_ref, o_ref)


def _merge_ffn(x, p_main, ya, yb, yc, wb, wo, ng, g1, w1, w2, g2, layer):
    bsz, s, d = x.shape
    w = BRANCH_W
    dff = w1.shape[-1]
    y_spec = pl.BlockSpec((1, MG_TM, w), lambda b, i: (b, i, 0))

    def gate_spec(n):
        return pl.BlockSpec((1, MG_TM, d), lambda b, i: (b, i, COL_GATE // d + n))

    return pl.pallas_call(
        _merge_ffn_kernel,
        grid=(bsz, s // MG_TM),
        in_specs=[
            pl.BlockSpec((1, MG_TM, d), lambda b, i: (b, i, 0)),
            gate_spec(0), gate_spec(1), gate_spec(2),
            y_spec, y_spec, y_spec,
            _layer_spec((N_BRANCH, w, d), layer),
            _layer_spec((d, d), layer),
            _const_spec((1, d)),
            _const_spec((1, d)),
            _layer_spec((d, dff), layer),
            _layer_spec((dff, d), layer),
            _const_spec((1, d)),
        ],
        out_specs=pl.BlockSpec((1, MG_TM, d), lambda b, i: (b, i, 0)),
        out_shape=jax.ShapeDtypeStruct((bsz, s, d), F32),
        scratch_shapes=[pltpu.VMEM((1, MG_TM, d), F32)],
        compiler_params=_cparams(("parallel", "parallel"), 60),
        name="merge_ffn",
    )(x, p_main, p_main, p_main, ya, yb, yc, wb, wo, ng, g1, w1, w2, g2)


def _lane_row(vals, start):
    return jnp.zeros((1, LANES), F32).at[0, start:start + vals.shape[0]].set(vals.astype(F32))


def kernel(x, norm_g, w_in, b_in, sgu_ln_g, sgu_ln_b, w_spatial, b_spatial, conv_w, a_log, dt_bias,
           dn_norm_g, w_branch, w_out, w_ff1, w_ff2):
    depth = norm_g.shape[0]
    x = x.astype(F32)
    w_in_b = w_in.astype(BF16)
    b_in_p = b_in.astype(F32)[:, None, :]
    w_branch_b, w_out_b = w_branch.astype(BF16), w_out.astype(BF16)
    w_ff1_b, w_ff2_b = w_ff1.astype(BF16), w_ff2.astype(BF16)
    for l in range(depth):
        p_main, bd, kt = _in_proj(x, norm_g[l, 0][None, :], w_in_b, b_in_p, l)

        bs_full = jnp.repeat(b_spatial[l].T, BRANCH_W // GM_GROUPS, axis=1)
        y_a = _sgu(p_main, w_spatial[l], bs_full, sgu_ln_g[l][None, :], sgu_ln_b[l][None, :])

        y_b = _deltanet(p_main, bd, conv_w[l], _lane_row(a_log[l], DN_HEADS),
                        _lane_row(dt_bias[l], DN_HEADS), dn_norm_g[l][None, :])

        y_c = _sb_attn(p_main, kt)

        x = _merge_ffn(x, p_main, y_a, y_b, y_c, w_branch_b, w_out_b, norm_g[l, 1][None, :],
                       norm_g[l, 2][None, :], w_ff1_b, w_ff2_b, norm_g[l, 3][None, :], l)
    return x
```

```python
import functools

import jax
import jax.numpy as jnp
from jax import lax
from jax.experimental import pallas as pl
from jax.experimental.pallas import tpu as pltpu

F32 = jnp.float32
BF16 = jnp.bfloat16
NORM_EPS = 1e-6

LANES = 128
MIB = 1024 * 1024

BRANCH_W = 512
GM_CHUNK = 128
GM_GROUPS = 8
DN_HEADS = 4
DN_DIM = 128
DN_CHUNK = 64
CONV_W = 4
SB_HEADS = 8
SB_DIM = 64
N_BRANCH = 3

COL_AU, COL_AV, COL_BQ, COL_BK, COL_BV, COL_BZ, COL_CQ, COL_CV, COL_GATE = (
    0, 512, 1024, 1536, 2048, 2560, 3072, 3584, 4096)


SB_T = 256
LOG2E = 1.4426950408889634
SB_SKIP = 100.0 * LOG2E
SB_GROUP = 4


def _cparams(sem, vmem_mib):
    return pltpu.CompilerParams(dimension_semantics=sem, vmem_limit_bytes=vmem_mib * MIB)


def _const_spec(shape):
    nd = len(shape)
    return pl.BlockSpec(shape, lambda *_: (0,) * nd, pipeline_mode=pl.Buffered(1))


def _layer_spec(shape, layer):
    nd = len(shape)
    return pl.BlockSpec((None,) + tuple(shape), lambda *_: (layer,) + (0,) * nd, pipeline_mode=pl.Buffered(1))


def _rms(x, g):
    return x * lax.rsqrt(jnp.mean(x * x, axis=-1, keepdims=True) + NORM_EPS) * g


def _sigmoid(x):
    return 1.0 / (1.0 + jnp.exp(-x))


def _softplus(x):
    return jnp.maximum(x, 0.0) + jnp.log(1.0 + jnp.exp(-jnp.abs(x)))


def _gelu(x):
    return 0.5 * x * (1.0 + lax.erf(x * (2.0 ** -0.5)))


def _dot(a, b):
    return jnp.dot(a, b, preferred_element_type=F32)


def _dot_nt(a, b):
    return lax.dot_general(a, b, (((1,), (1,)), ((), ())), preferred_element_type=F32)


IN_TM = 512
IN_CW = 1024


SRC_BD = 6 * BRANCH_W
SRC_CQ = SRC_BD + 2 * DN_HEADS
SRC_CK = SRC_CQ + BRANCH_W
SRC_CV = SRC_CK + BRANCH_W
SRC_GATE = SRC_CV + BRANCH_W
TAIL_CK = COL_GATE - SRC_BD + N_BRANCH * 2 * BRANCH_W
TAIL_W = TAIL_CK + BRANCH_W
REALIGN_ROWS = 128


def _in_proj_kernel(x_ref, g_ref, w_ref, b_ref, p_ref, bd_ref, kt_ref, wt_ref, bt_ref):
    d = w_ref.shape[0]
    gate_w = N_BRANCH * 2 * BRANCH_W
    moves = ((SRC_CQ, 0, BRANCH_W, SB_DIM ** -0.5), (SRC_CV, BRANCH_W, BRANCH_W, 1.0),
             (SRC_GATE, COL_GATE - SRC_BD, gate_w, 1.0), (SRC_CK, TAIL_CK, BRANCH_W, 1.0))

    def shifted(ref, rows, src, width):
        lo = src // LANES * LANES
        hi = min(-(-(src + width) // LANES) * LANES, ref.shape[-1])
        return ref[rows, lo:hi].astype(F32)[:, src - lo:src - lo + width]

    @pl.when((pl.program_id(0) == 0) & (pl.program_id(1) == 0))
    def _():
        for src, dst, width, scale in moves:
            bt_ref[:, dst:dst + width] = shifted(b_ref, slice(None), src, width) * scale

            def body(r, carry, src=src, dst=dst, width=width, scale=scale):
                rows = pl.ds(pl.multiple_of(r * REALIGN_ROWS, REALIGN_ROWS), REALIGN_ROWS)
                wt_ref[rows, dst:dst + width] = (shifted(w_ref, rows, src, width) * scale).astype(BF16)
                return carry

            lax.fori_loop(0, d // REALIGN_ROWS, body, 0)

    hb = _rms(x_ref[0], g_ref[...]).astype(BF16)
    for c in range(0, SRC_BD, IN_CW):
        acc = _dot(hb, w_ref[:, c:c + IN_CW]) + b_ref[:, c:c + IN_CW]
        p_ref[0, :, c:c + IN_CW] = acc.astype(BF16)
    bd_ref[0] = _dot(hb, w_ref[:, SRC_BD:SRC_BD + LANES]) + b_ref[:, SRC_BD:SRC_BD + LANES]
    for c in range(0, TAIL_CK, IN_CW):
        acc = _dot(hb, wt_ref[:, c:c + IN_CW]) + bt_ref[:, c:c + IN_CW]
        p_ref[0, :, SRC_BD + c:SRC_BD + c + IN_CW] = acc.astype(BF16)
    kt = (_dot(hb, wt_ref[:, TAIL_CK:TAIL_W]) + bt_ref[:, TAIL_CK:TAIL_W]).T
    for hp in range(SB_HEADS // 2):
        for jb in range(IN_TM // SB_T):
            kt_ref[0, hp, jb] = kt[hp * LANES:(hp + 1) * LANES,
                                   jb * SB_T:(jb + 1) * SB_T].astype(BF16)


def _in_proj(x, g, w, b, layer):
    bsz, s, d = x.shape
    p_in = w.shape[-1]
    pm = SRC_BD + TAIL_CK
    n_hp = SB_HEADS // 2
    grid = (bsz, s // IN_TM)
    return pl.pallas_call(
        _in_proj_kernel,
        grid=grid,
        in_specs=[
            pl.BlockSpec((1, IN_TM, d), lambda b, i: (b, i, 0)),
            _const_spec((1, d)),
            _layer_spec((d, p_in), layer),
            _layer_spec((1, p_in), layer),
        ],
        out_specs=[
            pl.BlockSpec((1, IN_TM, pm), lambda b, i: (b, i, 0)),
            pl.BlockSpec((1, IN_TM, LANES), lambda b, i: (b, i, 0)),
            pl.BlockSpec((1, n_hp, IN_TM // SB_T, LANES, SB_T), lambda b, i: (b, 0, i, 0, 0)),
        ],
        out_shape=[
            jax.ShapeDtypeStruct((bsz, s, pm), BF16),
            jax.ShapeDtypeStruct((bsz, s, LANES), F32),
            jax.ShapeDtypeStruct((bsz, n_hp, s // SB_T, LANES, SB_T), BF16),
        ],
        scratch_shapes=[pltpu.VMEM((d, TAIL_W), BF16), pltpu.VMEM((1, TAIL_W), F32)],
        compiler_params=_cparams(("arbitrary", "arbitrary"), 60),
        name="in_proj",
    )(x, g, w, b)


SGU_T = 1024


def _sgu_kernel(u_ref, v_ref, ws_ref, bs_ref, lng_ref, lnb_ref, o_ref):
    u = _gelu(u_ref[0].astype(F32))
    v = _gelu(v_ref[0].astype(F32))
    vc = v - jnp.mean(v, axis=-1, keepdims=True)
    v = vc * lax.rsqrt(jnp.mean(vc * vc, axis=-1, keepdims=True) + NORM_EPS)
    v = (v * lng_ref[...] + lnb_ref[...]).astype(BF16)
    row = lax.broadcasted_iota(jnp.int32, (GM_CHUNK, GM_CHUNK), 0)
    col = lax.broadcasted_iota(jnp.int32, (GM_CHUNK, GM_CHUNK), 1)
    causal = col <= row
    first_half = col < (LANES // 2)
    ws = [jnp.where(causal, ws_ref[g], 0.0).astype(BF16) for g in range(GM_GROUPS)]
    for c in range(SGU_T // GM_CHUNK):
        r0 = c * GM_CHUNK
        for p in range(GM_GROUPS // 2):
            v2 = v[r0:r0 + GM_CHUNK, p * LANES:(p + 1) * LANES]
            mixed = jnp.where(first_half, _dot(ws[2 * p], v2), _dot(ws[2 * p + 1], v2))
            mixed = mixed + bs_ref[:, p * LANES:(p + 1) * LANES]
            y = u[r0:r0 + GM_CHUNK, p * LANES:(p + 1) * LANES] * mixed
            o_ref[0, r0:r0 + GM_CHUNK, p * LANES:(p + 1) * LANES] = y.astype(BF16)


def _sgu(p_main, ws, bs_full, lng, lnb):
    bsz, s, _ = p_main.shape
    w = BRANCH_W
    return pl.pallas_call(
        _sgu_kernel,
        grid=(bsz, s // SGU_T),
        in_specs=[
            pl.BlockSpec((1, SGU_T, w), lambda b, i: (b, i, COL_AU // w)),
            pl.BlockSpec((1, SGU_T, w), lambda b, i: (b, i, COL_AV // w)),
            _const_spec((GM_GROUPS, GM_CHUNK, GM_CHUNK)),
            _const_spec((GM_CHUNK, w)),
            _const_spec((1, w)),
            _const_spec((1, w)),
        ],
        out_specs=pl.BlockSpec((1, SGU_T, w), lambda b, i: (b, i, 0)),
        out_shape=jax.ShapeDtypeStruct((bsz, s, w), BF16),
        compiler_params=_cparams(("parallel", "parallel"), 32),
        name="sgu",
    )(p_main, p_main, ws, bs_full, lng, lnb)


DN_T = 256
DN_SUB = 32
HALO = 8


def _dn_kernel(q_ref, k_ref, v_ref, z_ref, qp_ref, kp_ref, vp_ref, bd_ref,
               cw_ref, alog_ref, dtb_ref, ng_ref, o_ref, state_ref):
    sb = pl.program_id(0)
    nb = q_ref.shape[0]

    @pl.when(sb == 0)
    def _():
        state_ref[...] = jnp.zeros_like(state_ref)

    not_first = (sb > 0).astype(F32)

    halo_row = lax.broadcasted_iota(jnp.int32, (HALO, BRANCH_W), 0)

    def conv_silu(x_ref, xp_ref, part, b):
        x = x_ref[b].astype(F32)
        xp = xp_ref[b].astype(F32) * not_first
        taps = cw_ref[:, part * BRANCH_W:(part + 1) * BRANCH_W]
        y = taps[CONV_W - 1:CONV_W] * x
        for sft in range(1, CONV_W):
            xr = pltpu.roll(x, sft, 0)
            head = jnp.where(halo_row < sft, pltpu.roll(xp, sft, 0), xr[:HALO])
            xs = jnp.concatenate([head, xr[HALO:]], axis=0)
            y = y + taps[CONV_W - 1 - sft:CONV_W - sft] * xs
        return y * _sigmoid(y)

    q = [conv_silu(q_ref, qp_ref, 0, b) for b in range(nb)]
    k = [conv_silu(k_ref, kp_ref, 1, b) for b in range(nb)]
    v = [conv_silu(v_ref, vp_ref, 2, b) for b in range(nb)]
    z = [z_ref[b].astype(F32) for b in range(nb)]

    lane = lax.broadcasted_iota(jnp.int32, (DN_T, LANES), 1)
    decay_lanes = (lane >= DN_HEADS) & (lane < 2 * DN_HEADS)
    beta_all, g_all = [], []
    for b in range(nb):
        bd = bd_ref[b]
        beta_all.append(_sigmoid(bd))
        g_all.append(jnp.where(decay_lanes, -jnp.exp(alog_ref[...]) * _softplus(bd + dtb_ref[...]), 0.0))

    t = DN_T
    c = DN_CHUNK
    nc = t // c
    row = lax.broadcasted_iota(jnp.int32, (t, t), 0)
    col = lax.broadcasted_iota(jnp.int32, (t, t), 1)
    same_chunk = (row // c) == (col // c)
    tri = same_chunk & (col <= row)
    strict = same_chunk & (col < row)
    same_sub = (row // DN_SUB) == (col // DN_SUB)
    eye = (col == row).astype(F32)
    col_k =lax.broadcasted_iota(jnp.int32, (DN_DIM, t), 1) // c

    l_tri = tri.astype(BF16)

    def chunk_cumsum(g):
        g1 = g.astype(BF16)
        r1 = g - g1.astype(F32)
        g2 = r1.astype(BF16)
        g3 = (r1 - g2.astype(F32)).astype(BF16)
        s3 = _dot(l_tri, jnp.concatenate([g1, g2, g3], axis=1))
        return s3[:, :LANES] + s3[:, LANES:2 * LANES] + s3[:, 2 * LANES:]

    gc = [chunk_cumsum(g_all[b]) for b in range(nb)]
    gct = [gc[b].T for b in range(nb)]

    heads = range(nb * DN_HEADS)
    qn, kn, vn, kb = [], [], [], []
    gcol, bcol, egc, decay, glast, k_tt = [], [], [], [], [], []
    for h in heads:
        b, hh = divmod(h, DN_HEADS)
        hs = slice(hh * DN_DIM, (hh + 1) * DN_DIM)
        qh, kh = q[b][:, hs], k[b][:, hs]
        qn.append(qh * lax.rsqrt(jnp.sum(qh * qh, axis=-1, keepdims=True) + NORM_EPS) * (DN_DIM ** -0.5))
        kn.append(kh * lax.rsqrt(jnp.sum(kh * kh, axis=-1, keepdims=True) + NORM_EPS))
        vn.append(v[b][:, hs])
        kb.append(kn[h].astype(BF16))
        gcol.append(gc[b][:, DN_HEADS + hh:DN_HEADS + hh + 1])
        grow = gct[b][DN_HEADS + hh:DN_HEADS + hh + 1, :]
        bcol.append(beta_all[b][:, hh:hh + 1])
        egc.append(jnp.exp(gcol[h]))
        decay.append(jnp.exp(jnp.where(tri, gcol[h] - grow, -1e30)))
        glast.append([gcol[h][ci * c + c - 1:ci * c + c, :] for ci in range(nc)])
        glast_rows = jnp.concatenate([jnp.broadcast_to(g, (c, 1)) for g in glast[h]], axis=0)
        k_tt.append((kn[h] * jnp.exp(glast_rows - gcol[h])).T)

    kk = [_dot_nt(kb[h], kb[h]) for h in heads]
    qk = [_dot_nt(qn[h].astype(BF16), kb[h]) for h in heads]
    qk = [jnp.where(tri, qk[h] * decay[h], 0.0) for h in heads]
    a = [jnp.where(strict, bcol[h] * kk[h] * decay[h], 0.0) for h in heads]
    a_d = [jnp.where(same_sub, a[h], 0.0) for h in heads]
    a_l = [jnp.where(same_sub, 0.0, a[h]).astype(BF16) for h in heads]
    x = [_dot(a_d[h].astype(BF16), a_d[h].astype(BF16)) for h in heads]
    t_d = [eye - a_d[h] for h in heads]
    for _ in range(DN_SUB.bit_length() - 3):
        both = [_dot(jnp.concatenate([x[h], t_d[h]], axis=0).astype(BF16), x[h].astype(BF16))
                for h in heads]
        x = [both[h][:t] for h in heads]
        t_d = [t_d[h] + both[h][t:] for h in heads]
    last = [_dot(t_d[h].astype(BF16), x[h].astype(BF16)) for h in heads]
    t_d = [(t_d[h] + last[h]).astype(BF16) for h in heads]
    rhs = [jnp.concatenate([vn[h] * bcol[h], kn[h] * (bcol[h] * egc[h])], axis=-1) for h in heads]
    sol = [_dot(t_d[h], rhs[h].astype(BF16)) for h in heads]
    for _ in range(c // DN_SUB - 1):
        below = [_dot(a_l[h], sol[h].astype(BF16)) for h in heads]
        sol = [_dot(t_d[h], (rhs[h] - below[h]).astype(BF16)) for h in heads]
    u_l = [sol[h][:, :DN_DIM] for h in heads]
    w_l = [sol[h][:, DN_DIM:] for h in heads]
    qd_l = [qn[h] * egc[h] for h in heads]

    state = [state_ref[h] for h in heads]
    for ci in range(nc):
        rs = slice(ci * c, (ci + 1) * c)
        r1 = [_dot(jnp.concatenate([w_l[h][rs], qd_l[h][rs]], axis=0).astype(BF16), state[h].astype(BF16))
              for h in heads]
        lhs2, v_pad = [], []
        for h in heads:
            pieces = []
            if ci > 0:
                pieces.append(jnp.zeros((ci * c, DN_DIM), F32))
            pieces.append(u_l[h][rs] - r1[h][:c])
            if ci < nc - 1:
                pieces.append(jnp.zeros(((nc - 1 - ci) * c, DN_DIM), F32))
            v_pad.append(jnp.concatenate(pieces, axis=0).astype(BF16))
            ktt_c = jnp.where(col_k == ci, k_tt[h], jnp.zeros_like(k_tt[h]))
            lhs2.append(jnp.concatenate([qk[h][rs, :], ktt_c], axis=0).astype(BF16))
        r2 = [_dot(lhs2[h], v_pad[h]) for h in heads]
        for h in heads:
            b, hh = divmod(h, DN_HEADS)
            hs = slice(hh * DN_DIM, (hh + 1) * DN_DIM)
            state[h] = state[h] * jnp.exp(glast[h][ci]) + r2[h][c:]
            o = _rms(r1[h][c:] + r2[h][:c], ng_ref[...])
            zh = z[b][rs, hs]
            o_ref[b, rs, hs] = (o * (zh * _sigmoid(zh))).astype(BF16)
    for h in heads:
        state_ref[h] = state[h]


def _deltanet(p_main, bd, conv_w, alog_l, dtb_l, ng):
    bsz, s, _ = p_main.shape
    w = BRANCH_W
    hb = DN_T // HALO

    def cur(col):
        return pl.BlockSpec((bsz, DN_T, w), lambda i: (0, i, col // w))

    def prev(col):
        return pl.BlockSpec((bsz, HALO, w), lambda i: (0, jnp.maximum(i * hb - 1, 0), col // w))

    return pl.pallas_call(
        _dn_kernel,
        grid=(s // DN_T,),
        in_specs=[
            cur(COL_BQ), cur(COL_BK), cur(COL_BV), cur(COL_BZ),
            prev(COL_BQ), prev(COL_BK), prev(COL_BV),
            pl.BlockSpec((bsz, DN_T, LANES), lambda i: (0, i, 0)),
            _const_spec((CONV_W, 3 * w)),
            _const_spec((1, LANES)),
            _const_spec((1, LANES)),
            _const_spec((1, DN_DIM)),
        ],
        out_specs=pl.BlockSpec((bsz, DN_T, w), lambda i: (0, i, 0)),
        out_shape=jax.ShapeDtypeStruct((bsz, s, w), BF16),
        scratch_shapes=[pltpu.VMEM((bsz * DN_HEADS, DN_DIM, DN_DIM), F32)],
        compiler_params=_cparams(("arbitrary",), 48),
        name="deltanet",
    )(p_main, p_main, p_main, p_main, p_main, p_main, p_main, bd, conv_w, alog_l, dtb_l, ng)


def _sb_kernel(q_ref, kt_ref, v_ref, o_ref):
    i = pl.program_id(2)
    t = SB_T
    pairs = range(SB_GROUP)
    lane = lax.broadcasted_iota(jnp.int32, (t, LANES), 1)
    head0 = lane < SB_DIM
    qs = []
    for p in pairs:
        q2 = q_ref[0, :, p * LANES:(p + 1) * LANES]
        zq = jnp.zeros_like(q2)
        qs.append(jnp.concatenate([jnp.where(head0, q2, zq), jnp.where(head0, zq, q2)], axis=0))
    row = lax.broadcasted_iota(jnp.int32, (t, t), 0)
    col = lax.broadcasted_iota(jnp.int32, (t, t), 1)
    after = (row > col).astype(BF16)
    causal = jnp.concatenate([col < row, col < row], axis=0)

    def sweep(j_lo, n, mask_last, carry, acc):
        rows = pl.ds(pl.multiple_of(j_lo * t, t), n * t)
        z = [_dot(qs[p], jnp.concatenate([kt_ref[0, p, j_lo + m] for m in range(n)], axis=1))
             for p in pairs]
        log_keep, log_beta = [], []
        for p in pairs:
            lks, lbs = [], []
            for m in range(n):
                y = z[p][:, m * t:(m + 1) * t] * LOG2E
                ny = -y
                lk = jnp.minimum(ny, 0.0) - jnp.log2(1.0 + jnp.exp2(jnp.minimum(y, ny)))
                lbs.append(y + lk)
                if m == n - 1 and mask_last is not None:
                    lk = jnp.where(mask_last, lk, 0.0)
                lks.append(lk)
            log_keep.append(lks)
            log_beta.append(lbs)
        later = [_dot(jnp.concatenate(log_keep[p], axis=0).astype(BF16), after)
                 for p in pairs]
        w, new_carry = [], []
        for p in pairs:
            cp = carry[p]
            ws = [None] * n
            for m in reversed(range(n)):
                wp = jnp.exp2(log_beta[p][m] + later[p][m * 2 * t:(m + 1) * 2 * t] + cp)
                if m == n - 1 and mask_last is not None:
                    wp = jnp.where(mask_last, wp, 0.0)
                ws[m] = wp
                cp = cp + jnp.sum(log_keep[p][m], axis=1, keepdims=True)
            w.append(jnp.concatenate(ws, axis=1).astype(BF16))
            new_carry.append(cp)
        pv = [_dot(w[p], v_ref[0, rows, p * LANES:(p + 1) * LANES]) for p in pairs]
        return tuple(new_carry), tuple(acc[p] + pv[p] for p in pairs)

    def write(acc):
        for p in pairs:
            o_ref[0, :, p * LANES:(p + 1) * LANES] = jnp.where(head0, acc[p][:t], acc[p][t:]).astype(BF16)

    zero_carry = (jnp.zeros((2 * t, 1), F32),) * SB_GROUP
    zero_acc = (jnp.zeros((2 * t, LANES), F32),) * SB_GROUP

    @pl.when(i == 0)
    def _():
        write(sweep(0, 1, causal, zero_carry, zero_acc)[1])

    @pl.when(i > 0)
    def _():
        carry, acc = sweep(i - 1, 2, causal, zero_carry, zero_acc)

        def cond(st):
            j, carry, _ = st
            top = functools.reduce(jnp.maximum, [jnp.max(cp) for cp in carry])
            return jnp.logical_and(j >= 0, top > -SB_SKIP)

        def body(st):
            j, carry, acc = st
            carry, acc = sweep(j, 1, None, carry, acc)
            return j - 1, carry, acc

        write(lax.while_loop(cond, body, (i - 2, carry, acc))[2])


def _sb_attn(p_main, kt):
    bsz, s, _ = p_main.shape
    n_grp = SB_HEADS // 2 // SB_GROUP
    gw = SB_GROUP * LANES
    nb = s // SB_T
    return pl.pallas_call(
        _sb_kernel,
        grid=(bsz, n_grp, nb),
        in_specs=[
            pl.BlockSpec((1, SB_T, gw), lambda b, h, i: (b, i, COL_CQ // gw + h)),
            pl.BlockSpec((1, SB_GROUP, nb, LANES, SB_T), lambda b, h, i: (b, h, 0, 0, 0),
                         pipeline_mode=pl.Buffered(1)),
            pl.BlockSpec((1, s, gw), lambda b, h, i: (b, 0, COL_CV // gw + h), pipeline_mode=pl.Buffered(1)),
        ],
        out_specs=pl.BlockSpec((1, SB_T, gw), lambda b, h, i: (b, i, h)),
        out_shape=jax.ShapeDtypeStruct((bsz, s, BRANCH_W), BF16),
        compiler_params=_cparams(("parallel", "parallel", "arbitrary"), 48),
        name="sb_attn",
    )(p_main, kt, p_main)


MG_TM = 512


def _merge_kernel(x_ref, ga_ref, gb_ref, gc_ref, ya_ref, yb_ref, yc_ref, wb_ref, wo_ref, ng_ref, o_ref):
    m = None
    for n, (g_ref, y_ref) in enumerate(((ga_ref, ya_ref), (gb_ref, yb_ref), (gc_ref, yc_ref))):
        proj = _dot(y_ref[0], wb_ref[n])
        gate = _sigmoid(g_ref[0].astype(F32))
        m = gate * proj if m is None else m + gate * proj
    mixed = _dot(m.astype(BF16), wo_ref[...])
    o_ref[0] = x_ref[0] + _rms(mixed, ng_ref[...])


FF_CW = 1024


def _ffn_kernel(x_ref, g1_ref, w1_ref, w2_ref, g2_ref, o_ref):
    x = x_ref[0]
    hb = _rms(x, g1_ref[...]).astype(BF16)
    dff = w1_ref.shape[1]
    f = None
    for c in range(0, dff, FF_CW):
        a = jnp.maximum(_dot(hb, w1_ref[:, c:c + FF_CW]), 0.0)
        part = _dot((a * a).astype(BF16), w2_ref[c:c + FF_CW, :])
        f = part if f is None else f + part
    o_ref[0] = x + _rms(f, g2_ref[...])


def _merge_ffn_kernel(x_ref, ga_ref, gb_ref, gc_ref, ya_ref, yb_ref, yc_ref, wb_ref, wo_ref, ng_ref,
                      g1_ref, w1_ref, w2_ref, g2_ref, o_ref, xm_ref):
    _merge_kernel(x_ref, ga_ref, gb_ref, gc_ref, ya_ref, yb_ref, yc_ref, wb_ref, wo_ref, ng_ref, xm_ref)
    _ffn_kernel(xm_ref, g1_ref, w1_ref, w2_ref, g2_ref, o_ref)


def _merge_ffn(x, p_main, ya, yb, yc, wb, wo, ng, g1, w1, w2, g2, layer):
    bsz, s, d = x.shape
    w = BRANCH_W
    dff = w1.shape[-1]
    y_spec = pl.BlockSpec((1, MG_TM, w), lambda b, i: (b, i, 0))

    def gate_spec(n):
        return pl.BlockSpec((1, MG_TM, d), lambda b, i: (b, i, COL_GATE // d + n))

    return pl.pallas_call(
        _merge_ffn_kernel,
        grid=(bsz, s // MG_TM),
        in_specs=[
            pl.BlockSpec((1, MG_TM, d), lambda b, i: (b, i, 0)),
            gate_spec(0), gate_spec(1), gate_spec(2),
            y_spec, y_spec, y_spec,
            _layer_spec((N_BRANCH, w, d), layer),
            _layer_spec((d, d), layer),
            _const_spec((1, d)),
            _const_spec((1, d)),
            _layer_spec((d, dff), layer),
            _layer_spec((dff, d), layer),
            _const_spec((1, d)),
        ],
        out_specs=pl.BlockSpec((1, MG_TM, d), lambda b, i: (b, i, 0)),
        out_shape=jax.ShapeDtypeStruct((bsz, s, d), F32),
        scratch_shapes=[pltpu.VMEM((1, MG_TM, d), F32)],
        compiler_params=_cparams(("parallel", "parallel"), 60),
        name="merge_ffn",
    )(x, p_main, p_main, p_main, ya, yb, yc, wb, wo, ng, g1, w1, w2, g2)


def _lane_row(vals, start):
    return jnp.zeros((1, LANES), F32).at[0, start:start + vals.shape[0]].set(vals.astype(F32))


def kernel(x, norm_g, w_in, b_in, sgu_ln_g, sgu_ln_b, w_spatial, b_spatial, conv_w, a_log, dt_bias,
           dn_norm_g, w_branch, w_out, w_ff1, w_ff2):
    depth = norm_g.shape[0]
    x = x.astype(F32)
    w_in_b = w_in.astype(BF16)
    b_in_p = b_in.astype(F32)[:, None, :]
    w_branch_b, w_out_b = w_branch.astype(BF16), w_out.astype(BF16)
    w_ff1_b, w_ff2_b = w_ff1.astype(BF16), w_ff2.astype(BF16)
    for l in range(depth):
        p_main, bd, kt = _in_proj(x, norm_g[l, 0][None, :], w_in_b, b_in_p, l)

        bs_full = jnp.repeat(b_spatial[l].T, BRANCH_W // GM_GROUPS, axis=1)
        y_a = _sgu(p_main, w_spatial[l], bs_full, sgu_ln_g[l][None, :], sgu_ln_b[l][None, :])

        y_b = _deltanet(p_main, bd, conv_w[l], _lane_row(a_log[l], DN_HEADS),
                        _lane_row(dt_bias[l], DN_HEADS), dn_norm_g[l][None, :])

        y_c = _sb_attn(p_main, kt)

        x = _merge_ffn(x, p_main, y_a, y_b, y_c, w_branch_b, w_out_b, norm_g[l, 1][None, :],
                       norm_g[l, 2][None, :], w_ff1_b, w_ff2_b, norm_g[l, 3][None, :], l)
    return x
```
